```python
import math
import jax, jax.numpy as jnp
from jax import lax
import numpy as np

D_MODEL = 4096
BATCH = 2
SEQ = 4096
DEPTH = 2
DEC_BATCH = 2
DEC_SEQ = 8192
PAST_LEN = 128

HEAD_DIM = 128
N_HEAD_SLOTS = D_MODEL // HEAD_DIM
A_HEADS = 3 * N_HEAD_SLOTS // 8
A_KV_HEADS = A_HEADS // 3
B_HEADS = 3 * N_HEAD_SLOTS // 8
B_KV_HEADS = B_HEADS // 3
C_HEADS = N_HEAD_SLOTS // 8
A_WIDTH = A_HEADS * HEAD_DIM
B_WIDTH = B_HEADS * HEAD_DIM
C_WIDTH = C_HEADS * 2 * HEAD_DIM
MIX_WIDTH = A_WIDTH + B_WIDTH + C_WIDTH
IN_WIDTH = (A_HEADS + 2 * A_KV_HEADS) * HEAD_DIM + (B_HEADS + 2 * B_KV_HEADS) * HEAD_DIM + 3 * C_WIDTH

GRID_W = 64
AXIS_DIM = HEAD_DIM // 2
ROPE_THETA = 10000.0
QBLOCK = 128
B_BRANCHES = ((128, 1), (512, 4), (2048, 16))

NUM_BUCKETS = 32
MAX_DISTANCE = 128
BIAS_HEADS = B_HEADS + C_HEADS

N_GROUPS = 8
EXPERTS_PER_GROUP = 8
N_EXPERTS = N_GROUPS * EXPERTS_PER_GROUP
TOP_K = 2
D_EXPERT = D_MODEL // 4
MOE_BLOCK = 128

NORM_EPS = 1e-6
NEG_INF = -1e30

kernel_name = 'hybrid_bidir_encoder_hier_moe'


def rms_norm(x, g):
    xf = x.astype(jnp.float32)
    y = xf * lax.rsqrt(jnp.mean(xf * xf, axis=-1, keepdims=True) + NORM_EPS)
    return (y * g.astype(jnp.float32)).astype(x.dtype)


def to_blocks(x):
    b, s = x.shape[:2]
    return jnp.moveaxis(x.reshape((b, s // QBLOCK, QBLOCK) + x.shape[2:]), 1, 0)


def from_blocks(x):
    nb, b, q = x.shape[:3]
    return jnp.moveaxis(x, 0, 1).reshape((b, nb * q) + x.shape[3:])


def t5_bucket(rel):
    half = NUM_BUCKETS // 2
    max_exact = half // 2
    n = jnp.abs(rel)
    large = max_exact + (jnp.log(jnp.maximum(n, max_exact).astype(jnp.float32) / max_exact)
                         / math.log(MAX_DISTANCE / max_exact) * (half - max_exact)).astype(jnp.int32)
    large = jnp.minimum(large, half - 1)
    return jnp.where(rel > 0, half, 0) + jnp.where(n < max_exact, n, large)


def axial_rope_tables(s):
    rows = s // GRID_W
    row = jnp.repeat(jnp.arange(rows, dtype=jnp.float32), GRID_W)
    col = jnp.tile(jnp.arange(GRID_W, dtype=jnp.float32), rows)
    inv_freq = jnp.exp(-math.log(ROPE_THETA) * jnp.arange(0, AXIS_DIM, 2, dtype=jnp.float32) / AXIS_DIM)
    ang_r = row[:, None] * inv_freq[None, :]
    ang_c = col[:, None] * inv_freq[None, :]
    return (jnp.cos(ang_r), jnp.sin(ang_r), jnp.cos(ang_c), jnp.sin(ang_c))


def apply_axial_rope(x, rope):
    cos_r, sin_r, cos_c, sin_c = rope
    shape = (x.shape[1],) + (1,) * (x.ndim - 3) + (AXIS_DIM // 2,)

    def rot(u, cos, sin):
        cos = cos.reshape(shape).astype(u.dtype)
        sin = sin.reshape(shape).astype(u.dtype)
        u1, u2 = jnp.split(u, 2, axis=-1)
        return jnp.concatenate([u1 * cos - u2 * sin, u2 * cos + u1 * sin], axis=-1)

    return jnp.concatenate([rot(x[..., :AXIS_DIM], cos_r, sin_r), rot(x[..., AXIS_DIM:], cos_c, sin_c)], axis=-1)


def global_gqa(q, k, v):
    scale = HEAD_DIM ** -0.5

    def block(qb):
        sc = jnp.einsum('bqhgd,bkhd->bhgqk', qb, k, preferred_element_type=jnp.float32) * scale
        p = jax.nn.softmax(sc, axis=-1)
        return jnp.einsum('bhgqk,bkhd->bqhgd', p.astype(v.dtype), v)

    return from_blocks(lax.map(block, to_blocks(q)))


def dilated_attention(q, k, v, bias_tab):
    bsz, s, hkv, grp, hd = q.shape
    scale = hd ** -0.5
    branches = []
    for window, dil in B_BRANCHES:
        half = window // (2 * dil)
        offs = (jnp.arange(2 * half + 1, dtype=jnp.int32) - half) * dil
        bias = bias_tab[t5_bucket(offs)].T.reshape(hkv, grp, 2 * half + 1).astype(jnp.float32)
        branches.append((offs, bias))

    def block(args):
        qb, bi = args
        qpos = bi * QBLOCK + jnp.arange(QBLOCK, dtype=jnp.int32)
        lses, outs = [], []
        for offs, bias in branches:
            kpos = qpos[:, None] + offs[None, :]
            valid = (kpos >= 0) & (kpos < s)
            kpos = jnp.clip(kpos, 0, s - 1)
            kg = jnp.take(k, kpos, axis=1)
            vg = jnp.take(v, kpos, axis=1)
            sc = jnp.einsum('bqhgd,bqjhd->bhgqj', qb, kg, preferred_element_type=jnp.float32) * scale
            sc = jnp.where(valid, sc + bias[:, :, None, :], NEG_INF)
            lse = jax.nn.logsumexp(sc, axis=-1)
            p = jnp.exp(sc - lse[..., None])
            outs.append(jnp.einsum('bhgqj,bqjhd->bqhgd', p.astype(vg.dtype), vg))
            lses.append(lse)
        wts = jax.nn.softmax(jnp.stack(lses), axis=0)
        return jnp.einsum('nbhgq,nbqhgd->bqhgd', wts.astype(qb.dtype), jnp.stack(outs))

    o = lax.map(block, (to_blocks(q), jnp.arange(s // QBLOCK, dtype=jnp.int32)))
    return from_blocks(o)


def diff_attention(q, k, v, lam, bias_tab, lambda_init):
    s = q.shape[1]
    scale = HEAD_DIM ** -0.5
    lam = lam.astype(jnp.float32)
    lam_val = jnp.exp(jnp.sum(lam[0] * lam[1])) - jnp.exp(jnp.sum(lam[2] * lam[3])) + lambda_init
    kpos = jnp.arange(s, dtype=jnp.int32)

    def block(args):
        qb, bi = args
        qpos = bi * QBLOCK + jnp.arange(QBLOCK, dtype=jnp.int32)
        bias = jnp.moveaxis(bias_tab[t5_bucket(kpos[None, :] - qpos[:, None])], -1, 0).astype(jnp.float32)
        sc = jnp.einsum('bqhmd,bkhmd->bhmqk', qb, k, preferred_element_type=jnp.float32) * scale + bias[None, :, None]
        p = jax.nn.softmax(sc, axis=-1)
        a = p[:, :, 0] - lam_val * p[:, :, 1]
        return jnp.einsum('bhqk,bkhe->bqhe', a.astype(v.dtype), v)

    o = lax.map(block, (to_blocks(q), jnp.arange(s // QBLOCK, dtype=jnp.int32)))
    return from_blocks(o)


def token_mixer(h, rope, rel_bias, w_in, g_qk, lam, g_out, w_out, lambda_init):
    bsz, s, _ = h.shape
    sizes = [A_HEADS * HEAD_DIM, A_KV_HEADS * HEAD_DIM, A_KV_HEADS * HEAD_DIM,
             B_HEADS * HEAD_DIM, B_KV_HEADS * HEAD_DIM, B_KV_HEADS * HEAD_DIM,
             C_WIDTH, C_WIDTH, C_WIDTH]
    cuts, acc = [], 0
    for n in sizes[:-1]:
        acc += n
        cuts.append(acc)
    qa, ka, va, qb, kb, vb, qc, kc, vc = jnp.split(h @ w_in, cuts, axis=-1)
    ga = A_HEADS // A_KV_HEADS
    qa = apply_axial_rope(rms_norm(qa.reshape(bsz, s, A_KV_HEADS, ga, HEAD_DIM), g_qk[0]), rope)
    ka = apply_axial_rope(rms_norm(ka.reshape(bsz, s, A_KV_HEADS, HEAD_DIM), g_qk[1]), rope)
    oa = global_gqa(qa, ka, va.reshape(bsz, s, A_KV_HEADS, HEAD_DIM)).reshape(bsz, s, A_WIDTH)
    gb = B_HEADS // B_KV_HEADS
    ob = dilated_attention(qb.reshape(bsz, s, B_KV_HEADS, gb, HEAD_DIM),
                           kb.reshape(bsz, s, B_KV_HEADS, HEAD_DIM),
                           vb.reshape(bsz, s, B_KV_HEADS, HEAD_DIM),
                           rel_bias[:, :B_HEADS]).reshape(bsz, s, B_WIDTH)
    oc = diff_attention(qc.reshape(bsz, s, C_HEADS, 2, HEAD_DIM), kc.reshape(bsz, s, C_HEADS, 2, HEAD_DIM),
                        vc.reshape(bsz, s, C_HEADS, 2 * HEAD_DIM), lam, rel_bias[:, B_HEADS:], lambda_init)
    oa = rms_norm(oa, g_out[:A_WIDTH])
    ob = rms_norm(ob, g_out[A_WIDTH:A_WIDTH + B_WIDTH])
    oc = rms_norm(oc, g_out[A_WIDTH + B_WIDTH:].reshape(C_HEADS, 2 * HEAD_DIM)) * (1.0 - lambda_init)
    return jnp.concatenate([oa, ob, oc.reshape(bsz, s, C_WIDTH)], axis=-1) @ w_out


def grouped_expert_ffn(t, expert_idx, gates, w_gate, w_up, w_down):
    n_tok, d = t.shape
    n_assign = n_tok * TOP_K
    flat_e = expert_idx.reshape(-1)
    flat_tok = jnp.repeat(jnp.arange(n_tok, dtype=jnp.int32), TOP_K)
    order = jnp.argsort(flat_e)
    sorted_e = flat_e[order]
    counts = jnp.bincount(flat_e, length=N_EXPERTS)
    padded = (counts + MOE_BLOCK - 1) // MOE_BLOCK * MOE_BLOCK
    start = jnp.cumsum(counts) - counts
    pad_end = jnp.cumsum(padded)
    pad_start = pad_end - padded
    dest = (pad_start[sorted_e] + jnp.arange(n_assign, dtype=jnp.int32) - start[sorted_e]).astype(jnp.int32)
    n_rows = (n_assign + MOE_BLOCK - 1) // MOE_BLOCK * MOE_BLOCK + N_EXPERTS * MOE_BLOCK
    n_blocks = n_rows // MOE_BLOCK
    row_tok = jnp.zeros((n_rows,), jnp.int32).at[dest].set(flat_tok[order])
    block_expert = jnp.minimum(jnp.searchsorted(pad_end, jnp.arange(n_blocks) * MOE_BLOCK, side='right'),
                               N_EXPERTS - 1)

    def expert_block(args):
        tok, e = args
        xb = t[tok]
        hb = jax.nn.silu(xb @ w_gate[e]) * (xb @ w_up[e])
        return hb @ w_down[e]

    yb = lax.map(expert_block, (row_tok.reshape(n_blocks, MOE_BLOCK), block_expert)).reshape(n_rows, d)
    dest_orig = jnp.zeros((n_assign,), jnp.int32).at[order].set(dest)
    return jnp.einsum('tkd,tk->td', yb[dest_orig].reshape(n_tok, TOP_K, d), gates.astype(yb.dtype))


def hierarchical_moe(h, w_group, b_group, w_router, b_router, w_gate, w_up, w_down):
    bsz, s, d = h.shape
    t = h.reshape(bsz * s, d)
    g_logits = jnp.dot(t, w_group, preferred_element_type=jnp.float32) + b_group.astype(jnp.float32)
    g_prob = jax.nn.softmax(g_logits, axis=-1)
    grp = jnp.argmax(g_logits, axis=-1).astype(jnp.int32)
    p_grp = jnp.take_along_axis(g_prob, grp[:, None], axis=-1)
    e_logits = (jnp.dot(t, w_router, preferred_element_type=jnp.float32)
                + b_router.astype(jnp.float32)).reshape(-1, N_GROUPS, EXPERTS_PER_GROUP)
    e_logits = jnp.take_along_axis(e_logits, grp[:, None, None], axis=1)[:, 0]
    top_v, top_i = lax.top_k(e_logits, TOP_K)
    gates = p_grp * jax.nn.softmax(top_v, axis=-1)
    expert_idx = grp[:, None] * EXPERTS_PER_GROUP + top_i.astype(jnp.int32)
    return grouped_expert_ffn(t, expert_idx, gates, w_gate, w_up, w_down).reshape(bsz, s, d)


def encoder_trunk(x, c, rel_bias, w_ada, b_ada, g_norm1, w_in, g_qk, lam_c, g_out, w_out, g_norm2,
                  w_group, b_group, w_router, b_router, w_gate, w_up, w_down, g_final):
    rope = axial_rope_tables(x.shape[1])
    cs = jax.nn.silu(c)
    for l in range(DEPTH):
        lambda_init = 0.8 - 0.6 * math.exp(-0.3 * l)
        mod = (cs @ w_ada[l] + b_ada[l])[:, None, :]
        sh1, sc1, gt1, sh2, sc2, gt2 = jnp.split(mod, 6, axis=-1)
        h = rms_norm(x, g_norm1[l]) * (1 + sc1) + sh1
        x = x + gt1 * token_mixer(h, rope, rel_bias, w_in[l], g_qk[l], lam_c[l], g_out[l], w_out[l], lambda_init)
        h = rms_norm(x, g_norm2[l]) * (1 + sc2) + sh2
        x = x + gt2 * hierarchical_moe(h, w_group[l], b_group[l], w_router[l], b_router[l],
                                       w_gate[l], w_up[l], w_down[l])
    return rms_norm(x, g_final)


def setup_inputs(seed: int = 0) -> dict:
    key = jax.random.key(seed)
    ks = jax.random.split(key, 24)
    f32 = jnp.float32

    def nrm(k, shape, scale):
        return jax.random.normal(k, shape, f32) * scale

    def gain(k, shape):
        return 1.0 + 0.05 * jax.random.normal(k, shape, f32)

    return {
        'x_prompt': nrm(ks[0], (BATCH, SEQ, D_MODEL), 1.0),
        'x_sample': nrm(ks[1], (DEC_BATCH, DEC_SEQ, D_MODEL), 1.0),
        'c_prompt': nrm(ks[2], (BATCH, D_MODEL), 1.0),
        'c_sample': nrm(ks[3], (DEC_BATCH, D_MODEL), 1.0),
        'rel_bias': nrm(ks[4], (NUM_BUCKETS, BIAS_HEADS), 0.2),
        'w_ada': nrm(ks[5], (DEPTH, D_MODEL, 6 * D_MODEL), D_MODEL ** -0.5),
        'b_ada': nrm(ks[6], (DEPTH, 6 * D_MODEL), 0.02),
        'g_norm1': gain(ks[7], (DEPTH, D_MODEL)),
        'w_in': nrm(ks[8], (DEPTH, D_MODEL, IN_WIDTH), D_MODEL ** -0.5),
        'g_qk': gain(ks[9], (DEPTH, 2, HEAD_DIM)),
        'lam_c': nrm(ks[10], (DEPTH, 4, HEAD_DIM), 0.1),
        'g_out': gain(ks[11], (DEPTH, MIX_WIDTH)),
        'w_out': nrm(ks[12], (DEPTH, MIX_WIDTH, D_MODEL), MIX_WIDTH ** -0.5),
        'g_norm2': gain(ks[13], (DEPTH, D_MODEL)),
        'w_group': nrm(ks[14], (DEPTH, D_MODEL, N_GROUPS), D_MODEL ** -0.5),
        'b_group': nrm(ks[15], (DEPTH, N_GROUPS), 0.01),
        'w_router': nrm(ks[16], (DEPTH, D_MODEL, N_EXPERTS), D_MODEL ** -0.5),
        'b_router': nrm(ks[17], (DEPTH, N_EXPERTS), 0.01),
        'w_gate': nrm(ks[18], (DEPTH, N_EXPERTS, D_MODEL, D_EXPERT), D_MODEL ** -0.5),
        'w_up': nrm(ks[19], (DEPTH, N_EXPERTS, D_MODEL, D_EXPERT), D_MODEL ** -0.5),
        'w_down': nrm(ks[20], (DEPTH, N_EXPERTS, D_EXPERT, D_MODEL), D_EXPERT ** -0.5),
        'g_final': gain(ks[21], (D_MODEL,)),
    }


def reference(x_prompt, x_sample, c_prompt, c_sample, rel_bias, w_ada, b_ada, g_norm1, w_in, g_qk, lam_c,
              g_out, w_out, g_norm2, w_group, b_group, w_router, b_router, w_gate, w_up, w_down, g_final):
    y_prompt = encoder_trunk(x_prompt, c_prompt, rel_bias, w_ada, b_ada, g_norm1, w_in, g_qk, lam_c, g_out, w_out,
                             g_norm2, w_group, b_group, w_router, b_router, w_gate, w_up, w_down, g_final)
    y_sample = encoder_trunk(x_sample, c_sample, rel_bias, w_ada, b_ada, g_norm1, w_in, g_qk, lam_c, g_out, w_out,
                             g_norm2, w_group, b_group, w_router, b_router, w_gate, w_up, w_down, g_final)
    return (y_prompt, y_sample)
```

```python
import functools
import math

import jax
import jax.numpy as jnp
from jax import lax
from jax.experimental import pallas as pl
from jax.experimental.pallas import tpu as pltpu

F32 = jnp.float32
BF16 = jnp.bfloat16

HEAD_DIM = 128
LANES = 128
GRID_W = 64
AXIS_DIM = HEAD_DIM // 2
ROPE_THETA = 10000.0
B_BRANCHES = ((128, 1), (512, 4), (2048, 16))
B_HALF = 64
NUM_BUCKETS = 32
MAX_DISTANCE = 128
N_GROUPS = 8
EXPERTS_PER_GROUP = 8
N_EXPERTS = N_GROUPS * EXPERTS_PER_GROUP
TOP_K = 2
NORM_EPS = 1e-6
NEG_INF = -1e30
VMEM_LIMIT_BYTES = 56 * 1024 * 1024
MOE_BLOCK = 256


def _params(sem):
    return pltpu.CompilerParams(dimension_semantics=sem, vmem_limit_bytes=VMEM_LIMIT_BYTES)


class _Layout:
    def __init__(self, bp, sp, bs, ss):
        self.bp, self.sp, self.bs, self.ss = bp, sp, bs, ss
        self.tp = bp * sp
        self.t = bp * sp + bs * ss
        self.nseq = bp + bs
        self.groups = ((0, bp, sp), (self.tp, bs, ss))

    def seq_of(self, row0):
        return jnp.where(row0 < self.tp, row0 // self.sp, self.bp + (row0 - self.tp) // self.ss)

    def pos_of(self, row0):
        return jnp.where(row0 < self.tp, row0 % self.sp, (row0 - self.tp) % self.ss)


def _rms(x, g):
    var = jnp.mean(x * x, axis=-1, keepdims=True)
    return x * lax.rsqrt(var + NORM_EPS) * g


def _ada_kernel(c_ref, w_ref, b_ref, o_ref):
    c = c_ref[...]
    cs = c * jax.nn.sigmoid(c)
    o_ref[...] = jnp.dot(cs, w_ref[...], preferred_element_type=F32) + b_ref[...]


def _ada_mod(c_all, w_ada, b_ada):
    depth, d, n = w_ada.shape
    rows = c_all.shape[0]
    tn = min(512, n)
    return pl.pallas_call(
        _ada_kernel,
        grid=(depth, n // tn),
        in_specs=[
            pl.BlockSpec((rows, d), lambda l, j: (0, 0)),
            pl.BlockSpec((None, d, tn), lambda l, j: (l, 0, j)),
            pl.BlockSpec((None, 1, tn), lambda l, j: (l, 0, j)),
        ],
        out_specs=pl.BlockSpec((None, rows, tn), lambda l, j: (l, 0, j)),
        out_shape=jax.ShapeDtypeStruct((depth, rows, n), F32),
        compiler_params=_params(("parallel", "parallel")),
        name="ada_mod",
    )(c_all, w_ada, b_ada.reshape(depth, 1, n))


def _inproj_kernel(x_ref, g_ref, sc_ref, sh_ref, w_ref, o_ref, h_ref):
    @pl.when(pl.program_id(1) == 0)
    def _():
        h = _rms(x_ref[...], g_ref[...]) * (1.0 + sc_ref[...]) + sh_ref[...]
        h_ref[...] = h.astype(BF16)

    o_ref[...] = jnp.dot(h_ref[...], w_ref[...], preferred_element_type=F32).astype(o_ref.dtype)


def _inproj(x, g, sc, sh, w_bf16, lay):
    t, d = x.shape
    n = w_bf16.shape[1]
    tm, tn = min(512, lay.sp), min(1024, n)
    seq = lambda i, j: (lay.seq_of(i * tm), 0, 0)
    return pl.pallas_call(
        _inproj_kernel,
        grid=(t // tm, n // tn),
        in_specs=[
            pl.BlockSpec((tm, d), lambda i, j: (i, 0)),
            pl.BlockSpec((1, d), lambda i, j: (0, 0)),
            pl.BlockSpec((None, 1, d), seq),
            pl.BlockSpec((None, 1, d), seq),
            pl.BlockSpec((d, tn), lambda i, j: (0, j)),
        ],
        out_specs=pl.BlockSpec((tm, tn), lambda i, j: (i, j)),
        out_shape=jax.ShapeDtypeStruct((t, n), BF16),
        scratch_shapes=[pltpu.VMEM((tm, d), BF16)],
        compiler_params=_params(("parallel", "arbitrary")),
        name="in_proj",
    )(x, g.reshape(1, d), sc, sh, w_bf16)


def _aprep_kernel(p_ref, g_ref, cos_ref, sin_ref, o_ref):
    cos = cos_ref[...]
    sin = sin_ref[...]
    lane = lax.broadcasted_iota(jnp.int32, cos.shape, 1)
    first_half = (lane % AXIS_DIM) < (AXIS_DIM // 2)
    for h in range(p_ref.shape[1] // HEAD_DIM):
        sl = slice(h * HEAD_DIM, (h + 1) * HEAD_DIM)
        y = _rms(p_ref[:, sl].astype(F32), g_ref[:, sl])
        partner = jnp.where(first_half,
                            pltpu.roll(y, HEAD_DIM - AXIS_DIM // 2, 1),
                            pltpu.roll(y, AXIS_DIM // 2, 1))
        o_ref[:, sl] = (y * cos + partner * sin).astype(o_ref.dtype)


def _aprep(p, g_row, rope_cos, rope_sin, lay, width):
    t = p.shape[0]
    tm, cw = min(512, lay.sp), min(512, width)
    pos = lambda i, j: (lay.pos_of(i * tm) // tm, 0)
    return pl.pallas_call(
        _aprep_kernel,
        grid=(t // tm, width // cw),
        in_specs=[
            pl.BlockSpec((tm, cw), lambda i, j: (i, j)),
            pl.BlockSpec((1, cw), lambda i, j: (0, j)),
            pl.BlockSpec((tm, HEAD_DIM), pos),
            pl.BlockSpec((tm, HEAD_DIM), pos),
        ],
        out_specs=pl.BlockSpec((tm, cw), lambda i, j: (i, j)),
        out_shape=jax.ShapeDtypeStruct((t, width), BF16),
        compiler_params=_params(("parallel", "parallel")),
        name="a_prep",
    )(p, g_row, rope_cos, rope_sin)


def _flash_a_kernel(q_ref, k_ref, v_ref, prev_ref, o_ref, qs_ref, m_ref, l_ref, acc_ref, *, scale, grp):
    kb = pl.program_id(3)
    tq = q_ref.shape[0]

    @pl.when(kb == 0)
    def _():
        for g in range(grp):
            qs_ref[g * tq:(g + 1) * tq, :] = q_ref[:, g * HEAD_DIM:(g + 1) * HEAD_DIM]
        m_ref[...] = jnp.full(m_ref.shape, -jnp.inf, F32)
        l_ref[...] = jnp.zeros(l_ref.shape, F32)
        acc_ref[...] = jnp.zeros(acc_ref.shape, F32)

    s = lax.dot_general(qs_ref[...], k_ref[...], (((1,), (1,)), ((), ())),
                        preferred_element_type=F32)
    m_prev = m_ref[...]
    m_new = jnp.maximum(m_prev, jnp.max(s, axis=-1, keepdims=True))
    alpha = jnp.exp((m_prev - m_new) * scale)
    p = jnp.exp((s - m_new) * scale)
    l_ref[...] = alpha * l_ref[...] + jnp.sum(p, axis=-1, keepdims=True)
    acc_ref[...] = alpha * acc_ref[...] + jnp.dot(p.astype(BF16), v_ref[...],
                                                  preferred_element_type=F32)
    m_ref[...] = m_new

    @pl.when(kb == pl.num_programs(3) - 1)
    def _():
        out = acc_ref[...] / l_ref[...]
        for g in range(grp):
            o_ref[:, g * HEAD_DIM:(g + 1) * HEAD_DIM] = out[g * tq:(g + 1) * tq].astype(o_ref.dtype)


def _flash_a(qk, p, oa, group, dims):
    base, nb, s = group
    grp = dims["a_grp"]
    kvh = dims["a_kv"]
    tq, tk = min(256, s), min(512, s)
    qw = grp * HEAD_DIM
    k_col0 = dims["a_heads"]
    v_col0 = dims["va_off"] // HEAD_DIM
    kern = functools.partial(_flash_a_kernel, scale=HEAD_DIM ** -0.5, grp=grp)
    out = pl.pallas_call(
        kern,
        grid=(nb, kvh, s // tq, s // tk),
        in_specs=[
            pl.BlockSpec((tq, qw), lambda b, h, i, j: ((base + b * s) // tq + i, h)),
            pl.BlockSpec((tk, HEAD_DIM), lambda b, h, i, j: ((base + b * s) // tk + j, k_col0 + h)),
            pl.BlockSpec((tk, HEAD_DIM), lambda b, h, i, j: ((base + b * s) // tk + j, v_col0 + h)),
            pl.BlockSpec(memory_space=pl.ANY),
        ],
        out_specs=pl.BlockSpec((tq, qw), lambda b, h, i, j: ((base + b * s) // tq + i, h)),
        out_shape=jax.ShapeDtypeStruct(oa.shape, oa.dtype),
        scratch_shapes=[
            pltpu.VMEM((grp * tq, HEAD_DIM), BF16),
            pltpu.VMEM((grp * tq, 1), F32),
            pltpu.VMEM((grp * tq, 1), F32),
            pltpu.VMEM((grp * tq, HEAD_DIM), F32),
        ],
        input_output_aliases={3: 0},
        compiler_params=_params(("parallel", "parallel", "parallel", "arbitrary")),
        name="flash_a",
    )
    return out(qk, qk, p, oa)


def _dil_kernel(q0_ref, q1_ref, q2_ref, k_ref, v_ref, tab_ref, prev_o_ref, prev_l_ref,
                o_ref, lse_ref, *, scale):
    ib = pl.program_id(3)
    nib = pl.num_programs(3)
    tq = q0_ref.shape[0]
    win = tab_ref.shape[-1]
    n_d = k_ref.shape[0]
    ws = pl.multiple_of(jnp.clip(ib * tq - B_HALF, 0, n_d - win), B_HALF)
    variant = jnp.where(ib == 0, 1, jnp.where(ib == nib - 1, 2, 0))
    kw = k_ref[pl.ds(ws, win), :]
    vw = v_ref[pl.ds(ws, win), :]
    lane = lax.broadcasted_iota(jnp.int32, (tq, LANES), 1)
    lse_tile = jnp.zeros((tq, LANES), F32)
    for g, q_ref in enumerate((q0_ref, q1_ref, q2_ref)):
        s = lax.dot_general(q_ref[...], kw, (((1,), (1,)), ((), ())),
                            preferred_element_type=F32) * scale + tab_ref[variant, g]
        m = jnp.max(s, axis=-1, keepdims=True)
        p = jnp.exp(s - m)
        l = jnp.sum(p, axis=-1, keepdims=True)
        o = jnp.dot(p.astype(BF16), vw, preferred_element_type=F32) / l
        o_ref[:, g * HEAD_DIM:(g + 1) * HEAD_DIM] = o
        lse_tile = jnp.where(lane == g, m + jnp.log(l), lse_tile)
    lse_ref[...] = lse_tile


def _dilated_branch(p, tab, ob, lse, group, dims, dil):
    base, nb, s = group
    t, w2 = p.shape
    kv = dims["b_kv"]
    n_d = s // dil
    tq = 128
    assert n_d >= 2 * tq and base % s == 0 and tab.shape[-1] == 2 * tq
    cb = w2 // HEAD_DIM
    q_cb, k_cb, v_cb = (dims[n] // HEAD_DIM for n in ("qb_off", "kb_off", "vb_off"))
    pv = p.reshape(t // dil, dil * w2)
    obv = ob.reshape(t // dil, dil * ob.shape[1])
    lsev = lse.reshape(t // dil, dil * lse.shape[1])
    row = lambda b, i: (base // dil + b * n_d) // tq + i

    def qspec(g):
        return pl.BlockSpec((tq, HEAD_DIM), lambda b, h, r, i: (row(b, i), r * cb + q_cb + h * 3 + g))

    kern = functools.partial(_dil_kernel, scale=HEAD_DIM ** -0.5)
    o_new, l_new = pl.pallas_call(
        kern,
        grid=(nb, kv, dil, n_d // tq),
        in_specs=[
            qspec(0), qspec(1), qspec(2),
            pl.BlockSpec((n_d, HEAD_DIM), lambda b, h, r, i: (base // s + b, r * cb + k_cb + h)),
            pl.BlockSpec((n_d, HEAD_DIM), lambda b, h, r, i: (base // s + b, r * cb + v_cb + h)),
            pl.BlockSpec((None, 3, 3, tq, 2 * tq), lambda b, h, r, i: (h, 0, 0, 0, 0)),
            pl.BlockSpec(memory_space=pl.ANY),
            pl.BlockSpec(memory_space=pl.ANY),
        ],
        out_specs=[
            pl.BlockSpec((tq, 3 * HEAD_DIM), lambda b, h, r, i: (row(b, i), r * kv + h)),
            pl.BlockSpec((tq, LANES), lambda b, h, r, i: (row(b, i), r * kv + h)),
        ],
        out_shape=[jax.ShapeDtypeStruct(obv.shape, obv.dtype),
                   jax.ShapeDtypeStruct(lsev.shape, lsev.dtype)],
        input_output_aliases={6: 0, 7: 1},
        compiler_params=_params(("parallel", "parallel", "parallel", "arbitrary")),
        name=f"dilated_{dil}",
    )(pv, pv, pv, pv, pv, tab, obv, lsev)
    return o_new.reshape(ob.shape), l_new.reshape(lse.shape)


def _flash_c_kernel(par_ref, q_ref, k_ref, v_ref, tab_ref, g_ref, prev_ref, o_ref,
                    m_ref, l_ref, acc_ref, *, scale, out_scale):
    h = pl.program_id(1)
    delta = pl.program_id(3) - pl.program_id(2)
    nheads = pl.num_programs(1)

    @pl.when(pl.program_id(3) == 0)
    def _():
        m_ref[...] = jnp.full(m_ref.shape, -jnp.inf, F32)
        l_ref[...] = jnp.zeros(l_ref.shape, F32)
        acc_ref[...] = jnp.zeros(acc_ref.shape, F32)

    def step(bias, c):
        for mi in range(2):
            sl = slice(mi * HEAD_DIM, (mi + 1) * HEAD_DIM)
            s = lax.dot_general(q_ref[:, sl], k_ref[:, sl], (((1,), (1,)), ((), ())),
                                preferred_element_type=F32) * scale
            row_max = jnp.max(s, axis=-1, keepdims=True)
            if bias is not None:
                s = s + bias
                row_max = jnp.max(s, axis=-1, keepdims=True)
            else:
                row_max = row_max + c
            m_prev = m_ref[mi]
            m_new = jnp.maximum(m_prev, row_max)
            alpha = jnp.exp(m_prev - m_new)
            p = jnp.exp(s - (m_new if bias is not None else m_new - c))
            l_ref[mi] = alpha * l_ref[mi] + jnp.sum(p, axis=-1, keepdims=True)
            acc_ref[mi] = alpha * acc_ref[mi] + jnp.dot(p.astype(BF16), v_ref[...],
                                                        preferred_element_type=F32)
            m_ref[mi] = m_new

    near = jnp.abs(delta) <= 1

    @pl.when(near)
    def _():
        step(tab_ref[jnp.clip(delta, -1, 1) + 1], None)

    @pl.when(jnp.logical_not(near))
    def _():
        step(None, jnp.where(delta > 0, par_ref[h, 1], par_ref[h, 0]))

    @pl.when(pl.program_id(3) == pl.num_programs(3) - 1)
    def _():
        lam = par_ref[nheads, 0]
        o = acc_ref[0] / l_ref[0] - lam * (acc_ref[1] / l_ref[1])
        o_ref[...] = (_rms(o, g_ref[...]) * out_scale).astype(o_ref.dtype)


def _flash_c(p, par, tab, g_c, oc, group, dims, out_scale):
    base, nb, s = group
    heads = dims["c_heads"]
    t = tab.shape[-1]
    vw = 2 * HEAD_DIM
    q_cb, k_cb, v_cb = (dims[n] // vw for n in ("qc_off", "kc_off", "vc_off"))
    row = lambda b, i: (base + b * s) // t + i
    kern = functools.partial(_flash_c_kernel, scale=HEAD_DIM ** -0.5, out_scale=out_scale)
    return pl.pallas_call(
        kern,
        grid=(nb, heads, s // t, s // t),
        in_specs=[
            pl.BlockSpec(memory_space=pltpu.SMEM),
            pl.BlockSpec((t, vw), lambda b, h, i, j: (row(b, i), q_cb + h)),
            pl.BlockSpec((t, vw), lambda b, h, i, j: (row(b, j), k_cb + h)),
            pl.BlockSpec((t, vw), lambda b, h, i, j: (row(b, j), v_cb + h)),
            pl.BlockSpec((None, 3, t, t), lambda b, h, i, j: (h, 0, 0, 0)),
            pl.BlockSpec((None, 1, vw), lambda b, h, i, j: (h, 0, 0)),
            pl.BlockSpec(memory_space=pl.ANY),
        ],
        out_specs=pl.BlockSpec((t, vw), lambda b, h, i, j: (row(b, i), h)),
        out_shape=jax.ShapeDtypeStruct(oc.shape, oc.dtype),
        scratch_shapes=[
            pltpu.VMEM((2, t, 1), F32),
            pltpu.VMEM((2, t, 1), F32),
            pltpu.VMEM((2, t, vw), F32),
        ],
        input_output_aliases={6: 0},
        compiler_params=_params(("parallel", "parallel", "parallel", "arbitrary")),
        name="flash_c",
    )(par, p, p, p, tab, g_c, oc)


def _mix_kernel(oa_ref, o1_ref, o2_ref, o3_ref, l1_ref, l2_ref, l3_ref, oc_ref, g_ref, out_ref,
                *, kv):
    a_w = oa_ref.shape[1]
    b_w = o1_ref.shape[1]
    tm = oa_ref.shape[0]
    g = g_ref[...]
    out_ref[:, :a_w] = _rms(oa_ref[...], g[:, :a_w]).astype(out_ref.dtype)
    lane = lax.broadcasted_iota(jnp.int32, (tm, LANES), 1)
    heads = []
    for h in range(kv):
        tiles = [r[:, h * LANES:(h + 1) * LANES] for r in (l1_ref, l2_ref, l3_ref)]
        for gg in range(3):
            c = (h * 3 + gg) * HEAD_DIM
            ls = [jnp.sum(jnp.where(lane == gg, tl, 0.0), axis=-1, keepdims=True) for tl in tiles]
            mx = jnp.maximum(jnp.maximum(ls[0], ls[1]), ls[2])
            es = [jnp.exp(x - mx) for x in ls]
            den = es[0] + es[1] + es[2]
            heads.append(sum((e / den) * r[:, c:c + HEAD_DIM]
                             for e, r in zip(es, (o1_ref, o2_ref, o3_ref))))
    ob = jnp.concatenate(heads, axis=-1)
    out_ref[:, a_w:a_w + b_w] = _rms(ob, g[:, a_w:a_w + b_w]).astype(out_ref.dtype)
    out_ref[:, a_w + b_w:] = oc_ref[...]


def _mix(oa, obs, lses, oc, g_out, dims):
    t = oa.shape[0]
    d = g_out.shape[-1]
    tm = 256
    full = lambda a: pl.BlockSpec((tm, a.shape[1]), lambda i: (i, 0))
    return pl.pallas_call(
        functools.partial(_mix_kernel, kv=dims["b_kv"]),
        grid=(t // tm,),
        in_specs=[full(oa)] + [full(a) for a in obs] + [full(a) for a in lses]
                 + [full(oc), pl.BlockSpec((1, d), lambda i: (0, 0))],
        out_specs=pl.BlockSpec((tm, d), lambda i: (i, 0)),
        out_shape=jax.ShapeDtypeStruct((t, d), BF16),
        compiler_params=_params(("parallel",)),
        name="mix_norm",
    )(oa, *obs, *lses, oc, g_out.reshape(1, d))


def _outproj_kernel(a_ref, w_ref, x_ref, gt_ref, o_ref):
    acc = jnp.dot(a_ref[...], w_ref[...], preferred_element_type=F32)
    o_ref[...] = x_ref[...] + gt_ref[...] * acc


def _outproj(a, w_bf16, x, gt, lay):
    t, d = x.shape
    k = a.shape[1]
    tm, tn = min(512, lay.sp), min(1024, d)
    return pl.pallas_call(
        _outproj_kernel,
        grid=(t // tm, d // tn),
        in_specs=[
            pl.BlockSpec((tm, k), lambda i, j: (i, 0)),
            pl.BlockSpec((k, tn), lambda i, j: (0, j)),
            pl.BlockSpec((tm, tn), lambda i, j: (i, j)),
            pl.BlockSpec((None, 1, tn), lambda i, j: (lay.seq_of(i * tm), 0, j)),
        ],
        out_specs=pl.BlockSpec((tm, tn), lambda i, j: (i, j)),
        out_shape=jax.ShapeDtypeStruct((t, d), F32),
        compiler_params=_params(("parallel", "parallel")),
        name="out_proj",
    )(a, w_bf16, x, gt)


def _router_kernel(x_ref, g_ref, sc_ref, sh_ref, whi_ref, wlo_ref, b_ref, h_ref, gate_ref, idx_ref):
    h = _rms(x_ref[...], g_ref[...]) * (1.0 + sc_ref[...]) + sh_ref[...]
    h_hi = h.astype(BF16)
    h_lo = (h - h_hi.astype(F32)).astype(BF16)
    for c in range(h_ref.shape[1]):
        h_ref[:, c, :] = h_hi[:, c * LANES:(c + 1) * LANES]
    logits = (jnp.dot(h_hi, whi_ref[...], preferred_element_type=F32)
              + (jnp.dot(h_hi, wlo_ref[...], preferred_element_type=F32)
                 + jnp.dot(h_lo, whi_ref[...], preferred_element_type=F32))
              + b_ref[...])
    lane = lax.broadcasted_iota(jnp.int32, logits.shape, 1)
    neg = jnp.float32(-jnp.inf)
    is_grp = lane < N_GROUPS
    gl = jnp.where(is_grp, logits, neg)
    gmax = jnp.max(gl, axis=-1, keepdims=True)
    grp = jnp.min(jnp.where(gl == gmax, lane, LANES), axis=-1, keepdims=True)
    p_grp = 1.0 / jnp.sum(jnp.where(is_grp, jnp.exp(gl - gmax), 0.0), axis=-1, keepdims=True)
    lo = N_GROUPS + grp * EXPERTS_PER_GROUP
    el = jnp.where((lane >= lo) & (lane < lo + EXPERTS_PER_GROUP), logits, neg)
    t1 = jnp.max(el, axis=-1, keepdims=True)
    i1 = jnp.min(jnp.where(el == t1, lane, LANES), axis=-1, keepdims=True)
    el2 = jnp.where(lane == i1, neg, el)
    t2 = jnp.max(el2, axis=-1, keepdims=True)
    i2 = jnp.min(jnp.where(el2 == t2, lane, LANES), axis=-1, keepdims=True)
    e = jnp.exp(t2 - t1)
    w1 = p_grp / (1.0 + e)
    w2 = p_grp * e / (1.0 + e)
    gate_ref[...] = jnp.where(lane == 0, w1, jnp.where(lane == 1, w2, 0.0))
    idx_ref[...] = jnp.where(lane == 0, i1 - N_GROUPS, jnp.where(lane == 1, i2 - N_GROUPS, 0))


def _router(x, g, sc, sh, w_hi, w_lo, b_row, lay):
    t, d = x.shape
    tm = 256
    seq = lambda i: (lay.seq_of(i * tm), 0, 0)
    const = lambda i: (0, 0)
    return pl.pallas_call(
        _router_kernel,
        grid=(t // tm,),
        in_specs=[
            pl.BlockSpec((tm, d), lambda i: (i, 0)),
            pl.BlockSpec((1, d), const),
            pl.BlockSpec((None, 1, d), seq),
            pl.BlockSpec((None, 1, d), seq),
            pl.BlockSpec((d, LANES), const),
            pl.BlockSpec((d, LANES), const),
            pl.BlockSpec((1, LANES), const),
        ],
        out_specs=[pl.BlockSpec((tm, d // LANES, LANES), lambda i: (i, 0, 0)),
                   pl.BlockSpec((tm, LANES), lambda i: (i, 0)),
                   pl.BlockSpec((tm, LANES), lambda i: (i, 0))],
        out_shape=[jax.ShapeDtypeStruct((t, d // LANES, LANES), BF16),
                   jax.ShapeDtypeStruct((t, LANES), F32),
                   jax.ShapeDtypeStruct((t, LANES), jnp.int32)],
        compiler_params=_params(("parallel",)),
        name="moe_router",
    )(x, g.reshape(1, d), sc, sh, w_hi, w_lo, b_row)


def _gather_kernel(nblk_ref, cur_ref, nxt_ref, h_ref, o_ref, buf_ref, sem):
    i = pl.program_id(0)
    n = pl.num_programs(0)
    rb = buf_ref.shape[1]

    def copy(slot, j, tok):
        return pltpu.make_async_copy(h_ref.at[pl.ds(tok, 1)], buf_ref.at[slot, pl.ds(j, 1)], sem.at[slot])

    def start_all(idx_ref, slot):
        def body(j, c):
            copy(slot, j, idx_ref[0, 0, j]).start()
            return c
        lax.fori_loop(0, rb, body, 0)

    @pl.when(i == 0)
    def _():
        start_all(cur_ref, 0)

    @pl.when(jnp.logical_and(i + 1 < n, i + 1 < nblk_ref[0]))
    def _():
        start_all(nxt_ref, (i + 1) % 2)

    @pl.when(i < nblk_ref[0])
    def _():
        slot = i % 2

        def drain(j, c):
            copy(slot, j, 0).wait()
            return c
        lax.fori_loop(0, rb, drain, 0)
        for c in range(buf_ref.shape[2]):
            o_ref[:, c * LANES:(c + 1) * LANES] = buf_ref[slot, :, c, :]

    @pl.when(i >= nblk_ref[0])
    def _():
        o_ref[...] = jnp.zeros(o_ref.shape, o_ref.dtype)


def _gather_rows(h3, row_tok, n_used_blocks):
    n_rows = row_tok.shape[0]
    _, nc, _ = h3.shape
    rb = MOE_BLOCK
    nb = n_rows // rb
    tok_blocks = row_tok.reshape(nb, 1, rb)
    return pl.pallas_call(
        _gather_kernel,
        grid_spec=pltpu.PrefetchScalarGridSpec(
            num_scalar_prefetch=1,
            grid=(nb,),
            in_specs=[
                pl.BlockSpec((1, 1, rb), lambda i, nu: (i, 0, 0), memory_space=pltpu.SMEM),
                pl.BlockSpec((1, 1, rb), lambda i, nu: (jnp.minimum(i + 1, nb - 1), 0, 0),
                             memory_space=pltpu.SMEM),
                pl.BlockSpec(memory_space=pl.ANY),
            ],
            out_specs=pl.BlockSpec((rb, nc * LANES), lambda i, nu: (i, 0)),
            scratch_shapes=[pltpu.VMEM((2, rb, nc, LANES), h3.dtype), pltpu.SemaphoreType.DMA((2,))],
        ),
        out_shape=jax.ShapeDtypeStruct((n_rows, nc * LANES), h3.dtype),
        compiler_params=_params(("arbitrary",)),
        name="moe_gather",
    )(n_used_blocks, tok_blocks, tok_blocks, h3)


def _moe_up_kernel(be_ref, first_ref, nblk_ref, x_ref, wg_ref, wu_ref, o_ref, wgb_ref, wub_ref):
    blk = pl.program_id(1)

    @pl.when(blk < nblk_ref[0])
    def _():
        @pl.when(first_ref[blk] == 1)
        def _():
            wgb_ref[...] = wg_ref[...].astype(BF16)
            wub_ref[...] = wu_ref[...].astype(BF16)

        x = x_ref[...]
        g = jnp.dot(x, wgb_ref[...], preferred_element_type=F32)
        u = jnp.dot(x, wub_ref[...], preferred_element_type=F32)
        o_ref[...] = (g * jax.nn.sigmoid(g) * u).astype(o_ref.dtype)

    @pl.when(blk >= nblk_ref[0])
    def _():
        o_ref[...] = jnp.zeros(o_ref.shape, o_ref.dtype)


def _moe_down_kernel(be_ref, first_ref, nblk_ref, h_ref, wd_ref, o_ref, wdb_ref):
    blk = pl.program_id(1)

    @pl.when(blk < nblk_ref[0])
    def _():
        @pl.when(first_ref[blk] == 1)
        def _():
            wdb_ref[...] = wd_ref[...].astype(BF16)

        acc = jnp.dot(h_ref[...], wdb_ref[...], preferred_element_type=F32).astype(o_ref.dtype)
        for c in range(o_ref.shape[1]):
            o_ref[:, c, :] = acc[:, c * LANES:(c + 1) * LANES]

    @pl.when(blk >= nblk_ref[0])
    def _():
        o_ref[...] = jnp.zeros(o_ref.shape, o_ref.dtype)


def _moe_experts(xs, be, first, nblk, w_gate, w_up, w_down, layer):
    n_rows, d = xs.shape
    de = w_gate.shape[-1]
    blk = MOE_BLOCK
    nb = n_rows // blk
    tn1, tn2 = min(512, de), min(1024, d)
    rowblk = lambda b, nblk_ref: jnp.minimum(b, nblk_ref[0] - 1)
    hmid = pl.pallas_call(
        _moe_up_kernel,
        grid_spec=pltpu.PrefetchScalarGridSpec(
            num_scalar_prefetch=3,
            grid=(de // tn1, nb),
            in_specs=[
                pl.BlockSpec((blk, d), lambda n, b, be, fi, nu: (rowblk(b, nu), 0)),
                pl.BlockSpec((None, None, d, tn1), lambda n, b, be, fi, nu: (layer, be[b], 0, n)),
                pl.BlockSpec((None, None, d, tn1), lambda n, b, be, fi, nu: (layer, be[b], 0, n)),
            ],
            out_specs=pl.BlockSpec((blk, tn1), lambda n, b, be, fi, nu: (b, n)),
            scratch_shapes=[pltpu.VMEM((d, tn1), BF16), pltpu.VMEM((d, tn1), BF16)],
        ),
        out_shape=jax.ShapeDtypeStruct((n_rows, de), BF16),
        compiler_params=_params(("arbitrary", "arbitrary")),
        name="moe_up",
    )(be, first, nblk, xs, w_gate, w_up)
    return pl.pallas_call(
        _moe_down_kernel,
        grid_spec=pltpu.PrefetchScalarGridSpec(
            num_scalar_prefetch=3,
            grid=(d // tn2, nb),
            in_specs=[
                pl.BlockSpec((blk, de), lambda n, b, be, fi, nu: (rowblk(b, nu), 0)),
                pl.BlockSpec((None, None, de, tn2), lambda n, b, be, fi, nu: (layer, be[b], 0, n)),
            ],
            out_specs=pl.BlockSpec((blk, tn2 // LANES, LANES),
                                   lambda n, b, be, fi, nu: (b, n, 0)),
            scratch_shapes=[pltpu.VMEM((de, tn2), BF16)],
        ),
        out_shape=jax.ShapeDtypeStruct((n_rows, d // LANES, LANES), F32),
        compiler_params=_params(("arbitrary", "arbitrary")),
        name="moe_down",
    )(be, first, nblk, hmid, w_down)


def _combine_kernel(cur_ref, nxt_ref, y_ref, x_ref, gate_ref, gt_ref, gf_ref, o_ref, buf_ref, sem,
                    *, final_norm):
    i = pl.program_id(0)
    n = pl.num_programs(0)
    tm = x_ref.shape[0]
    rows = buf_ref.shape[1]

    def copy(slot, j, r):
        return pltpu.make_async_copy(y_ref.at[pl.ds(r, 1)], buf_ref.at[slot, pl.ds(j, 1)], sem.at[slot])

    def start_all(idx_ref, slot):
        def body(j, c):
            copy(slot, j, idx_ref[0, 0, j]).start()
            return c
        lax.fori_loop(0, rows, body, 0)

    @pl.when(i == 0)
    def _():
        start_all(cur_ref, 0)

    @pl.when(i + 1 < n)
    def _():
        start_all(nxt_ref, (i + 1) % 2)

    slot = i % 2

    def drain(j, c):
        copy(slot, j, 0).wait()
        return c
    lax.fori_loop(0, rows, drain, 0)

    gate = gate_ref[...]
    lane = lax.broadcasted_iota(jnp.int32, gate.shape, 1)
    w1 = jnp.sum(jnp.where(lane == 0, gate, 0.0), axis=-1, keepdims=True)
    w2 = jnp.sum(jnp.where(lane == 1, gate, 0.0), axis=-1, keepdims=True)
    moe = jnp.concatenate(
        [w1 * buf_ref[slot, pl.ds(0, tm), c, :] + w2 * buf_ref[slot, pl.ds(tm, tm), c, :]
         for c in range(buf_ref.shape[2])], axis=-1)
    out = x_ref[...] + gt_ref[...] * moe
    if final_norm:
        out = _rms(out, gf_ref[...])
    o_ref[...] = out


def _combine(y, dest_blocks, x, gates, gt, g_final, lay, final_norm):
    t, d = x.shape
    tm = 128
    nb = t // tm
    kern = functools.partial(_combine_kernel, final_norm=final_norm)
    return pl.pallas_call(
        kern,
        grid=(nb,),
        in_specs=[
            pl.BlockSpec((1, 1, 2 * tm), lambda i: (i, 0, 0), memory_space=pltpu.SMEM),
            pl.BlockSpec((1, 1, 2 * tm), lambda i: (jnp.minimum(i + 1, nb - 1), 0, 0),
                         memory_space=pltpu.SMEM),
            pl.BlockSpec(memory_space=pl.ANY),
            pl.BlockSpec((tm, d), lambda i: (i, 0)),
            pl.BlockSpec((tm, LANES), lambda i: (i, 0)),
            pl.BlockSpec((None, 1, d), lambda i: (lay.seq_of(i * tm), 0, 0)),
            pl.BlockSpec((1, d), lambda i: (0, 0)),
        ],
        out_specs=pl.BlockSpec((tm, d), lambda i: (i, 0)),
        out_shape=jax.ShapeDtypeStruct((t, d), F32),
        scratch_shapes=[pltpu.VMEM((2, 2 * tm, d // LANES, LANES), y.dtype),
                        pltpu.SemaphoreType.DMA((2,))],
        compiler_params=_params(("arbitrary",)),
        name="moe_combine",
    )(dest_blocks, dest_blocks, y, x, gates, gt, g_final.reshape(1, d))


def _t5_bucket(rel):
    half = NUM_BUCKETS // 2
    max_exact = half // 2
    n = jnp.abs(rel)
    large = max_exact + (jnp.log(jnp.maximum(n, max_exact).astype(F32) / max_exact)
                         / math.log(MAX_DISTANCE / max_exact) * (half - max_exact)).astype(jnp.int32)
    large = jnp.minimum(large, half - 1)
    return jnp.where(rel > 0, half, 0) + jnp.where(n < max_exact, n, large)


def _rope_tables(s):
    rows = s // GRID_W
    row = jnp.repeat(jnp.arange(rows, dtype=F32), GRID_W)
    col = jnp.tile(jnp.arange(GRID_W, dtype=F32), rows)
    inv_freq = jnp.exp(-math.log(ROPE_THETA) * jnp.arange(0, AXIS_DIM, 2, dtype=F32) / AXIS_DIM)
    ang_r = row[:, None] * inv_freq[None, :]
    ang_c = col[:, None] * inv_freq[None, :]
    cos = jnp.concatenate([jnp.cos(ang_r)] * 2 + [jnp.cos(ang_c)] * 2, axis=-1)
    sin = jnp.concatenate([-jnp.sin(ang_r), jnp.sin(ang_r), -jnp.sin(ang_c), jnp.sin(ang_c)], axis=-1)
    return cos, sin


def _dilated_tables(bias_tab, kv, dil):
    offs = (jnp.arange(2 * B_HALF + 1, dtype=jnp.int32) - B_HALF) * dil
    bias = bias_tab[_t5_bucket(offs)].astype(F32)
    tq = 128
    shift = jnp.array([-B_HALF, 0, -2 * B_HALF], jnp.int32)[:, None, None]
    rel = (jnp.arange(2 * tq, dtype=jnp.int32)[None, None, :] + shift
           - jnp.arange(tq, dtype=jnp.int32)[None, :, None])
    valid = jnp.abs(rel) <= B_HALF
    vals = bias[jnp.clip(rel + B_HALF, 0, 2 * B_HALF)]
    tab = jnp.where(valid[..., None], vals, NEG_INF)
    tab = jnp.moveaxis(tab, -1, 0).reshape(kv, 3, 3, tq, 2 * tq)
    return jnp.swapaxes(tab, 1, 2)


def _diff_tables(bias_tab, t):
    rel = (jnp.arange(t, dtype=jnp.int32)[None, None, :]
           + jnp.array([-t, 0, t], jnp.int32)[:, None, None]
           - jnp.arange(t, dtype=jnp.int32)[None, :, None])
    near = jnp.moveaxis(bias_tab[_t5_bucket(rel)].astype(F32), -1, 0)
    far = bias_tab[_t5_bucket(jnp.array([-2 * t, 2 * t], jnp.int32))].astype(F32).T
    return near, far


def _routing(idx, blk):
    t = idx.shape[0]
    n_assign = t * TOP_K
    flat_e = idx.reshape(-1)
    onehot = (flat_e[:, None] == jnp.arange(N_EXPERTS, dtype=jnp.int32)[None, :]).astype(jnp.int32)
    csum = jnp.cumsum(onehot, axis=0)
    rank = jnp.take_along_axis(csum, flat_e[:, None], axis=1)[:, 0] - 1
    counts = csum[-1]
    padded = (counts + blk - 1) // blk * blk
    pad_end = jnp.cumsum(padded)
    pad_start = pad_end - padded
    dest = (pad_start[flat_e] + rank).astype(jnp.int32)
    n_rows = n_assign + N_EXPERTS * blk
    n_blocks = n_rows // blk
    flat_tok = jnp.arange(n_assign, dtype=jnp.int32) // TOP_K
    row_tok = jnp.zeros((n_rows,), jnp.int32).at[dest].set(flat_tok)
    n_used = (pad_end[-1] // blk).astype(jnp.int32)
    blk_ids = jnp.minimum(jnp.arange(n_blocks, dtype=jnp.int32), n_used - 1)
    be = jnp.minimum(jnp.searchsorted(pad_end, blk_ids * blk, side="right"), N_EXPERTS - 1).astype(jnp.int32)
    first = jnp.concatenate([jnp.ones((1,), jnp.int32), (be[1:] != be[:-1]).astype(jnp.int32)])
    return dest.reshape(t, TOP_K), row_tok, be, first, n_used.reshape(1)


def _dims(d):
    slots = d // HEAD_DIM
    a_heads = 3 * slots // 8
    a_kv = a_heads // 3
    b_heads = 3 * slots // 8
    b_kv = b_heads // 3
    c_heads = slots // 8
    c_w = c_heads * 2 * HEAD_DIM
    offs, acc = [], 0
    for n in (a_heads, a_kv, a_kv, b_heads, b_kv, b_kv):
        offs.append(acc)
        acc += n * HEAD_DIM
    for n in (c_w, c_w, c_w):
        offs.append(acc)
        acc += n
    names = ("qa_off", "ka_off", "va_off", "qb_off", "kb_off", "vb_off", "qc_off", "kc_off", "vc_off")
    out = dict(zip(names, offs))
    out.update(a_heads=a_heads, a_kv=a_kv, a_grp=3, b_heads=b_heads, b_kv=b_kv, c_heads=c_heads,
               a_w=a_heads * HEAD_DIM, b_w=b_heads * HEAD_DIM, c_w=c_w, in_w=acc)
    return out


def kernel(x_prompt, x_sample, c_prompt, c_sample, rel_bias, w_ada, b_ada, g_norm1, w_in, g_qk, lam_c,
           g_out, w_out, g_norm2, w_group, b_group, w_router, b_router, w_gate, w_up, w_down, g_final):
    bp, sp, d = x_prompt.shape
    bs, ss, _ = x_sample.shape
    depth = w_in.shape[0]
    lay = _Layout(bp, sp, bs, ss)
    dims = _dims(d)
    t = lay.t
    x = jnp.concatenate([x_prompt.reshape(bp * sp, d), x_sample.reshape(bs * ss, d)], axis=0)

    c_all = jnp.concatenate([c_prompt, c_sample], axis=0)
    pad = (-c_all.shape[0]) % 8
    c_pad = jnp.pad(c_all, ((0, pad), (0, 0)))
    mod = _ada_mod(c_pad, w_ada, b_ada)[:, :lay.nseq].reshape(depth, lay.nseq, 6, 1, d)

    rope_cos, rope_sin = _rope_tables(max(sp, ss))
    qk_w = dims["va_off"]
    b_tabs = [_dilated_tables(rel_bias[:, :dims["b_heads"]], dims["b_kv"], dil) for _, dil in B_BRANCHES]
    c_t = min(512, sp, ss)
    c_near, c_far = _diff_tables(rel_bias[:, dims["b_heads"]:], c_t)

    for l in range(depth):
        lambda_init = 0.8 - 0.6 * math.exp(-0.3 * l)
        sh1, sc1, gt1, sh2, sc2, gt2 = (mod[l, :, i] for i in range(6))

        p = _inproj(x, g_norm1[l], sc1, sh1, w_in[l].astype(BF16), lay)
        g_row = jnp.concatenate([jnp.tile(g_qk[l, 0], dims["a_heads"]),
                                 jnp.tile(g_qk[l, 1], dims["a_kv"])]).reshape(1, qk_w)
        qk = _aprep(p, g_row, rope_cos, rope_sin, lay, qk_w)

        lam = lam_c[l].astype(F32)
        lam_val = (jnp.exp(jnp.sum(lam[0] * lam[1])) - jnp.exp(jnp.sum(lam[2] * lam[3])) + lambda_init)
        par = jnp.concatenate([c_far, jnp.stack([lam_val, jnp.zeros((), F32)])[None, :]], axis=0)
        g_c = g_out[l, dims["a_w"] + dims["b_w"]:].reshape(dims["c_heads"], 1, 2 * HEAD_DIM)

        oa = jnp.zeros((t, dims["a_w"]), F32)
        oc = jnp.zeros((t, dims["c_w"]), BF16)
        obs = [jnp.zeros((t, dims["b_w"]), F32) for _ in B_BRANCHES]
        lses = [jnp.zeros((t, dims["b_kv"] * LANES), F32) for _ in B_BRANCHES]
        for group in lay.groups:
            oa = _flash_a(qk, p, oa, group, dims)
            oc = _flash_c(p, par, c_near, g_c, oc, group, dims, 1.0 - lambda_init)
            for n, (_, dil) in enumerate(B_BRANCHES):
                obs[n], lses[n] = _dilated_branch(p, b_tabs[n], obs[n], lses[n], group, dims, dil)
        mixed = _mix(oa, obs, lses, oc, g_out[l], dims)
        x = _outproj(mixed, w_out[l].astype(BF16), x, gt1, lay)

        w_r = jnp.concatenate([w_group[l], w_router[l],
                               jnp.zeros((d, LANES - N_GROUPS - N_EXPERTS), F32)], axis=1)
        w_hi = w_r.astype(BF16)
        w_lo = (w_r - w_hi.astype(F32)).astype(BF16)
        b_row = jnp.concatenate([b_group[l], b_router[l],
                                 jnp.zeros((LANES - N_GROUPS - N_EXPERTS,), F32)]).reshape(1, LANES)
        h2, gates, idx = _router(x, g_norm2[l], sc2, sh2, w_hi, w_lo, b_row.astype(F32), lay)
        dest, row_tok, be, first, n_used = _routing(idx[:, :TOP_K], MOE_BLOCK)
        xs = _gather_rows(h2, row_tok, n_used)
        y = _moe_experts(xs, be, first, n_used, w_gate, w_up, w_down, l)
        tmc = 128
        dest_blocks = dest.reshape(t // tmc, tmc, TOP_K).transpose(0, 2, 1).reshape(t // tmc, 1, TOP_K * tmc)
        x = _combine(y, dest_blocks, x, gates, gt2, g_final, lay, final_norm=(l == depth - 1))

    y_prompt = x[:lay.tp].reshape(bp, sp, d)
    y_sample = x[lay.tp:].reshape(bs, ss, d)
    return (y_prompt, y_sample)
```

```python
import functools
import math

import jax
import jax.numpy as jnp
from jax import lax
from jax.experimental import pallas as pl
from jax.experimental.pallas import tpu as pltpu

F32 = jnp.float32
BF16 = jnp.bfloat16

HEAD_DIM = 128
LANES = 128
GRID_W = 64
AXIS_DIM = HEAD_DIM // 2
ROPE_THETA = 10000.0
B_BRANCHES = ((128, 1), (512, 4), (2048, 16))
B_HALF = 64
NUM_BUCKETS = 32
MAX_DISTANCE = 128
N_GROUPS = 8
EXPERTS_PER_GROUP = 8
N_EXPERTS = N_GROUPS * EXPERTS_PER_GROUP
TOP_K = 2
NORM_EPS = 1e-6
NEG_INF = -1e30
LOG2E = math.log2(math.e)
VMEM_LIMIT_BYTES = 56 * 1024 * 1024
MOE_BLOCK = 256
ROW_SLAB = 40


def _params(sem):
    return pltpu.CompilerParams(dimension_semantics=sem, vmem_limit_bytes=VMEM_LIMIT_BYTES)


class _Layout:
    def __init__(self, bp, sp, bs, ss):
        self.bp, self.sp, self.bs, self.ss = bp, sp, bs, ss
        self.tp = bp * sp
        self.t = bp * sp + bs * ss
        self.nseq = bp + bs
        self.groups = ((0, bp, sp), (self.tp, bs, ss))

    def seq_of(self, row0):
        return jnp.where(row0 < self.tp, row0 // self.sp, self.bp + (row0 - self.tp) // self.ss)

    def pos_of(self, row0):
        return jnp.where(row0 < self.tp, row0 % self.sp, (row0 - self.tp) % self.ss)


def _rms(x, g):
    var = jnp.mean(x * x, axis=-1, keepdims=True)
    return x * lax.rsqrt(var + NORM_EPS) * g


def _ada_kernel(c_ref, w_ref, b_ref, o_ref):
    c = c_ref[...]
    cs = c * jax.nn.sigmoid(c)
    o_ref[...] = jnp.dot(cs, w_ref[...], preferred_element_type=F32) + b_ref[...]


def _ada_mod(c_all, w_ada, b_ada):
    depth, d, n = w_ada.shape
    rows = c_all.shape[0]
    tn = min(512, n)
    return pl.pallas_call(
        _ada_kernel,
        grid=(depth, n // tn),
        in_specs=[
            pl.BlockSpec((rows, d), lambda l, j: (0, 0)),
            pl.BlockSpec((None, d, tn), lambda l, j: (l, 0, j)),
            pl.BlockSpec((None, 1, tn), lambda l, j: (l, 0, j)),
        ],
        out_specs=pl.BlockSpec((None, rows, tn), lambda l, j: (l, 0, j)),
        out_shape=jax.ShapeDtypeStruct((depth, rows, n), F32),
        compiler_params=_params(("parallel", "parallel")),
        name="ada_mod",
    )(c_all, w_ada, b_ada.reshape(depth, 1, n))


def _inproj_kernel(x_ref, g_ref, sc_ref, sh_ref, w_ref, o_ref, *rest, dils, nb_tiles):
    res_refs, (h_ref, acc_ref) = rest[:len(dils)], rest[len(dils):]
    j = pl.program_id(1)
    tm = x_ref.shape[0]

    @pl.when(j == 0)
    def _():
        h = _rms(x_ref[...], g_ref[...]) * (1.0 + sc_ref[...]) + sh_ref[...]
        h_ref[...] = h.astype(BF16)

    acc = jnp.dot(h_ref[...], w_ref[...], preferred_element_type=F32)
    o_ref[...] = acc.astype(o_ref.dtype)

    @pl.when(j < nb_tiles)
    def _():
        for c in range(acc_ref.shape[0]):
            cols = slice(c * LANES, (c + 1) * LANES)
            acc_ref[c] = acc[:, cols]
            for dil, r_ref in zip(dils, res_refs):
                for r in range(dil):
                    r_ref[r, :, cols] = acc_ref[c, pl.ds(r, tm // dil, stride=dil), :].astype(r_ref.dtype)


def _inproj(x, g, sc, sh, w_bf16, lay, dims):
    t, d = x.shape
    n = w_bf16.shape[1]
    tm, tn = min(512, lay.sp), d // 8
    bw = dims["qa_off"]
    nb_tiles = bw // tn
    assert nb_tiles * tn == bw
    dils = tuple(dil for _, dil in B_BRANCHES if dil > 1)
    seq = lambda i, j: (lay.seq_of(i * tm), 0, 0)
    kern = functools.partial(_inproj_kernel, dils=dils, nb_tiles=nb_tiles)
    return pl.pallas_call(
        kern,
        grid=(t // tm, n // tn),
        in_specs=[
            pl.BlockSpec((tm, d), lambda i, j: (i, 0)),
            pl.BlockSpec((1, d), lambda i, j: (0, 0)),
            pl.BlockSpec((None, 1, d), seq),
            pl.BlockSpec((None, 1, d), seq),
            pl.BlockSpec((d, tn), lambda i, j: (0, j)),
        ],
        out_specs=[pl.BlockSpec((tm, tn), lambda i, j: (i, j))]
                  + [pl.BlockSpec((dil, tm // dil, tn), lambda i, j: (0, i, jnp.minimum(j, nb_tiles - 1)))
                     for dil in dils],
        out_shape=[jax.ShapeDtypeStruct((t, n), BF16)]
                  + [jax.ShapeDtypeStruct((dil, t // dil, bw), BF16) for dil in dils],
        scratch_shapes=[pltpu.VMEM((tm, d), BF16), pltpu.VMEM((tn // LANES, tm, LANES), F32)],
        compiler_params=_params(("parallel", "arbitrary")),
        name="in_proj",
    )(x, g.reshape(1, d), sc, sh, w_bf16)


def _aprep_kernel(p_ref, g_ref, cos_ref, sin_ref, o_ref):
    cos = cos_ref[...]
    sin = sin_ref[...]
    lane = lax.broadcasted_iota(jnp.int32, cos.shape, 1)
    first_half = (lane % AXIS_DIM) < (AXIS_DIM // 2)
    for h in range(p_ref.shape[1] // HEAD_DIM):
        sl = slice(h * HEAD_DIM, (h + 1) * HEAD_DIM)
        y = _rms(p_ref[:, sl].astype(F32), g_ref[:, sl])
        partner = jnp.where(first_half,
                            pltpu.roll(y, HEAD_DIM - AXIS_DIM // 2, 1),
                            pltpu.roll(y, AXIS_DIM // 2, 1))
        o_ref[:, sl] = (y * cos + partner * sin).astype(o_ref.dtype)


def _aprep(p, g_row, rope_cos, rope_sin, lay, col0, width):
    t = p.shape[0]
    tm, cw = min(512, lay.sp), width // 4
    assert col0 % cw == 0
    pos = lambda i, j: (lay.pos_of(i * tm) // tm, 0)
    return pl.pallas_call(
        _aprep_kernel,
        grid=(t // tm, width // cw),
        in_specs=[
            pl.BlockSpec((tm, cw), lambda i, j: (i, col0 // cw + j)),
            pl.BlockSpec((1, cw), lambda i, j: (0, j)),
            pl.BlockSpec((tm, HEAD_DIM), pos),
            pl.BlockSpec((tm, HEAD_DIM), pos),
        ],
        out_specs=pl.BlockSpec((tm, cw), lambda i, j: (i, j)),
        out_shape=jax.ShapeDtypeStruct((t, width), BF16),
        compiler_params=_params(("parallel", "parallel")),
        name="a_prep",
    )(p, g_row, rope_cos, rope_sin)


def _flash_a_kernel(q_ref, k_ref, v_ref, prev_ref, o_ref, qs_ref, *, scale, grp, tk, unroll):
    tq = q_ref.shape[0]
    rows = grp * tq
    for g in range(grp):
        qs_ref[g * tq:(g + 1) * tq, :] = q_ref[:, g * HEAD_DIM:(g + 1) * HEAD_DIM]
    q = qs_ref[...]
    c = scale * LOG2E
    reps = tk // LANES

    def body(j, carry):
        m, l, acc = carry
        off = pl.multiple_of(j * tk, tk)
        s = lax.dot_general(q, k_ref[pl.ds(off, tk), :], (((1,), (1,)), ((), ())),
                            preferred_element_type=F32)
        m_new = jnp.maximum(m, jnp.max(s, axis=-1, keepdims=True))
        alpha = jnp.exp2((m - m_new) * c)
        p = jnp.exp2((s - jnp.tile(m_new, (1, reps))) * c)
        ps = p[:, :LANES]
        for i in range(1, reps):
            ps = ps + p[:, i * LANES:(i + 1) * LANES]
        acc = alpha * acc + jnp.dot(p.astype(BF16), v_ref[pl.ds(off, tk), :],
                                    preferred_element_type=F32)
        return m_new, alpha * l + ps, acc

    init = (jnp.full((rows, LANES), -jnp.inf, F32), jnp.zeros((rows, LANES), F32),
            jnp.zeros((rows, HEAD_DIM), F32))
    _, l, acc = lax.fori_loop(0, k_ref.shape[0] // tk, body, init, unroll=unroll)
    out = acc / jnp.sum(l, axis=-1, keepdims=True)
    for g in range(grp):
        o_ref[:, g * HEAD_DIM:(g + 1) * HEAD_DIM] = out[g * tq:(g + 1) * tq].astype(o_ref.dtype)


def _flash_a(qk, p, oa, group, dims):
    base, nb, s = group
    grp = dims["a_grp"]
    kvh = dims["a_kv"]
    tq, tk = min(256, s), min(2048, s)
    nk = s // tk
    qw = grp * HEAD_DIM
    k_col0 = dims["a_heads"]
    v_col0 = dims["va_off"] // HEAD_DIM
    kern = functools.partial(_flash_a_kernel, scale=HEAD_DIM ** -0.5, grp=grp, tk=tk,
                             unroll=2 if nk % 2 == 0 else 1)
    out = pl.pallas_call(
        kern,
        grid=(nb, kvh, s // tq),
        in_specs=[
            pl.BlockSpec((tq, qw), lambda b, h, i: ((base + b * s) // tq + i, h)),
            pl.BlockSpec((s, HEAD_DIM), lambda b, h, i: (base // s + b, k_col0 + h)),
            pl.BlockSpec((s, HEAD_DIM), lambda b, h, i: (base // s + b, v_col0 + h)),
            pl.BlockSpec(memory_space=pl.ANY),
        ],
        out_specs=pl.BlockSpec((tq, qw), lambda b, h, i: ((base + b * s) // tq + i, h)),
        out_shape=jax.ShapeDtypeStruct(oa.shape, oa.dtype),
        scratch_shapes=[pltpu.VMEM((grp * tq, HEAD_DIM), BF16)],
        input_output_aliases={3: 0},
        compiler_params=_params(("parallel", "parallel", "arbitrary")),
        name="flash_a",
    )
    return out(qk, qk, p, oa)


def _dil_kernel(q_ref, k_ref, v_ref, tab_ref, prev_o_ref, prev_l_ref, o_ref, lse_ref, *, scale, tq):
    nsub = q_ref.shape[0] // tq
    nib = pl.num_programs(3) * nsub
    win = tab_ref.shape[-1]
    n_d = k_ref.shape[0]
    lane = lax.broadcasted_iota(jnp.int32, (tq, LANES), 1)
    for u in range(nsub):
        ib = pl.program_id(3) * nsub + u
        rows = slice(u * tq, (u + 1) * tq)
        ws = pl.multiple_of(jnp.clip(ib * tq - B_HALF, 0, n_d - win), B_HALF)
        variant = jnp.where(ib == 0, 1, jnp.where(ib == nib - 1, 2, 0))
        kw = k_ref[pl.ds(ws, win), :]
        vw = v_ref[pl.ds(ws, win), :]
        lse_tile = jnp.zeros((tq, LANES), F32)
        for g in range(3):
            cols = slice(g * HEAD_DIM, (g + 1) * HEAD_DIM)
            s = lax.dot_general(q_ref[rows, cols], kw, (((1,), (1,)), ((), ())),
                                preferred_element_type=F32) * scale + tab_ref[variant, g]
            m = jnp.max(s, axis=-1, keepdims=True)
            p = jnp.exp(s - m)
            l = jnp.sum(p, axis=-1, keepdims=True)
            o_ref[rows, cols] = jnp.dot(p.astype(BF16), vw, preferred_element_type=F32) / l
            lse_tile = jnp.where(lane == g, m + jnp.log(l), lse_tile)
        lse_ref[rows, :] = lse_tile


def _dilated_branch(pd, tab, ob, lse, group, dims, dil):
    base, nb, s = group
    kv = dims["b_kv"]
    n_d = s // dil
    tq = 128
    tb = tq * min(4, n_d // tq)
    assert n_d >= 2 * tq and base % s == 0 and tab.shape[-1] == 2 * tq and pd.shape[0] == dil
    q_cb = dims["qb_off"] // (3 * HEAD_DIM)
    k_cb, v_cb = (dims[n] // HEAD_DIM for n in ("kb_off", "vb_off"))
    row = lambda b, i: (base // dil + b * n_d) // tb + i
    kern = functools.partial(_dil_kernel, scale=HEAD_DIM ** -0.5, tq=tq)
    return pl.pallas_call(
        kern,
        grid=(nb, kv, dil, n_d // tb),
        in_specs=[
            pl.BlockSpec((None, tb, 3 * HEAD_DIM), lambda b, h, r, i: (r, row(b, i), q_cb + h)),
            pl.BlockSpec((None, n_d, HEAD_DIM), lambda b, h, r, i: (r, base // s + b, k_cb + h)),
            pl.BlockSpec((None, n_d, HEAD_DIM), lambda b, h, r, i: (r, base // s + b, v_cb + h)),
            pl.BlockSpec((None, 3, 3, tq, 2 * tq), lambda b, h, r, i: (h, 0, 0, 0, 0)),
            pl.BlockSpec(memory_space=pl.ANY),
            pl.BlockSpec(memory_space=pl.ANY),
        ],
        out_specs=[
            pl.BlockSpec((None, tb, 3 * HEAD_DIM), lambda b, h, r, i: (r, row(b, i), h)),
            pl.BlockSpec((None, tb, LANES), lambda b, h, r, i: (r, row(b, i), h)),
        ],
        out_shape=[jax.ShapeDtypeStruct(ob.shape, ob.dtype), jax.ShapeDtypeStruct(lse.shape, lse.dtype)],
        input_output_aliases={4: 0, 5: 1},
        compiler_params=_params(("parallel", "parallel", "parallel", "arbitrary")),
        name=f"dilated_{dil}",
    )(pd, pd, pd, tab, ob, lse)


def _flash_c_kernel(lam_ref, q_ref, k_ref, v_ref, tab_ref, g_ref, prev_ref, o_ref,
                    *, scale, out_scale, unroll):
    qb = pl.program_id(2)
    t = q_ref.shape[0]
    c = scale * LOG2E
    reps = t // LANES
    qs = (q_ref[:, :HEAD_DIM], q_ref[:, HEAD_DIM:])

    def body(j, carry):
        off = pl.multiple_of(j * t, t)
        bias = tab_ref[jnp.clip(j - qb, -2, 2) + 2]
        v = v_ref[pl.ds(off, t), :]
        new = []
        for mi in range(2):
            m, l, acc = carry[mi]
            k = k_ref[pl.ds(off, t), mi * HEAD_DIM:(mi + 1) * HEAD_DIM]
            s = lax.dot_general(qs[mi], k, (((1,), (1,)), ((), ())),
                                preferred_element_type=F32) * c + bias
            m_new = jnp.maximum(m, jnp.max(s, axis=-1, keepdims=True))
            alpha = jnp.exp2(m - m_new)
            p = jnp.exp2(s - jnp.tile(m_new, (1, reps)))
            ps = p[:, :LANES]
            for i in range(1, reps):
                ps = ps + p[:, i * LANES:(i + 1) * LANES]
            acc = jnp.tile(alpha, (1, 2)) * acc + jnp.dot(p.astype(BF16), v,
                                                        preferred_element_type=F32)
            new.append((m_new, alpha * l + ps, acc))
        return tuple(new)

    one = (jnp.full((t, LANES), -jnp.inf, F32), jnp.zeros((t, LANES), F32),
           jnp.zeros((t, 2 * HEAD_DIM), F32))
    (_, l1, a1), (_, l2, a2) = lax.fori_loop(0, k_ref.shape[0] // t, body, (one, one), unroll=unroll)
    o = (a1 / jnp.sum(l1, axis=-1, keepdims=True)
         - lam_ref[0] * (a2 / jnp.sum(l2, axis=-1, keepdims=True)))
    o_ref[...] = (_rms(o, g_ref[...]) * out_scale).astype(o_ref.dtype)


def _flash_c(p, lam, tab, g_c, oc, group, dims, out_scale):
    base, nb, s = group
    heads = dims["c_heads"]
    t = tab.shape[-1]
    nk = s // t
    vw = 2 * HEAD_DIM
    q_cb, k_cb, v_cb = (dims[n] // vw for n in ("qc_off", "kc_off", "vc_off"))
    row = lambda b, i: (base + b * s) // t + i
    kern = functools.partial(_flash_c_kernel, scale=HEAD_DIM ** -0.5, out_scale=out_scale,
                             unroll=2 if nk % 2 == 0 else 1)
    return pl.pallas_call(
        kern,
        grid=(nb, heads, nk),
        in_specs=[
            pl.BlockSpec(memory_space=pltpu.SMEM),
            pl.BlockSpec((t, vw), lambda b, h, i: (row(b, i), q_cb + h)),
            pl.BlockSpec((s, vw), lambda b, h, i: (base // s + b, k_cb + h)),
            pl.BlockSpec((s, vw), lambda b, h, i: (base // s + b, v_cb + h)),
            pl.BlockSpec((None, 5, t, t), lambda b, h, i: (h, 0, 0, 0)),
            pl.BlockSpec((None, 1, vw), lambda b, h, i: (h, 0, 0)),
            pl.BlockSpec(memory_space=pl.ANY),
        ],
        out_specs=pl.BlockSpec((t, vw), lambda b, h, i: (row(b, i), h)),
        out_shape=jax.ShapeDtypeStruct(oc.shape, oc.dtype),
        input_output_aliases={6: 0},
        compiler_params=_params(("parallel", "parallel", "arbitrary")),
        name="flash_c",
    )(lam, p, p, p, tab, g_c, oc)


def _mix_kernel(oa_ref, *rest, kv, nbr):
    src_o, src_l = rest[:nbr], rest[nbr:2 * nbr]
    oc_ref, g_ref, out_ref = rest[2 * nbr:2 * nbr + 3]
    tok_o, tok_l = rest[2 * nbr + 3:3 * nbr + 3], rest[3 * nbr + 3:]
    a_w = oa_ref.shape[1]
    b_w = src_o[0].shape[-1]
    tm = oa_ref.shape[0]
    for src, dst in zip(src_o + src_l, tok_o + tok_l):
        dil = src.shape[0]
        for c in range(dst.shape[0]):
            for r in range(dil):
                dst[c, pl.ds(r, tm // dil, stride=dil), :] = src[r, :, c * LANES:(c + 1) * LANES]
    g = g_ref[...]
    out_ref[:, :a_w] = _rms(oa_ref[...], g[:, :a_w]).astype(out_ref.dtype)
    lane = lax.broadcasted_iota(jnp.int32, (tm, LANES), 1)
    heads = []
    for h in range(kv):
        tiles = [r[h] for r in tok_l]
        for gg in range(3):
            ls = [jnp.sum(jnp.where(lane == gg, tl, 0.0), axis=-1, keepdims=True) for tl in tiles]
            mx = jnp.maximum(jnp.maximum(ls[0], ls[1]), ls[2])
            es = [jnp.exp(x - mx) for x in ls]
            den = es[0] + es[1] + es[2]
            heads.append(sum((e / den) * r[h * 3 + gg] for e, r in zip(es, tok_o)))
    ob = jnp.concatenate(heads, axis=-1)
    out_ref[:, a_w:a_w + b_w] = _rms(ob, g[:, a_w:a_w + b_w]).astype(out_ref.dtype)
    out_ref[:, a_w + b_w:] = oc_ref[...]


def _mix(oa, obs, lses, oc, g_out, dims):
    t = oa.shape[0]
    d = g_out.shape[-1]
    tm = 256
    full = lambda a: pl.BlockSpec((tm, a.shape[1]), lambda i: (i, 0))
    res = lambda a: pl.BlockSpec((a.shape[0], tm // a.shape[0], a.shape[2]), lambda i: (0, i, 0))
    return pl.pallas_call(
        functools.partial(_mix_kernel, kv=dims["b_kv"], nbr=len(obs)),
        grid=(t // tm,),
        in_specs=[full(oa)] + [res(a) for a in obs] + [res(a) for a in lses]
                 + [full(oc), pl.BlockSpec((1, d), lambda i: (0, 0))],
        out_specs=pl.BlockSpec((tm, d), lambda i: (i, 0)),
        out_shape=jax.ShapeDtypeStruct((t, d), BF16),
        scratch_shapes=[pltpu.VMEM((a.shape[2] // LANES, tm, LANES), F32) for a in obs + lses],
        compiler_params=_params(("parallel",)),
        name="mix_norm",
    )(oa, *obs, *lses, oc, g_out.reshape(1, d))


def _outproj_kernel(a_ref, w_ref, x_ref, gt_ref, o_ref):
    acc = jnp.dot(a_ref[...], w_ref[...], preferred_element_type=F32)
    o_ref[...] = x_ref[...] + gt_ref[...] * acc


def _outproj(a, w_bf16, x, gt, lay):
    t, d = x.shape
    k = a.shape[1]
    tm, tn = min(512, lay.sp), min(1024, d)
    return pl.pallas_call(
        _outproj_kernel,
        grid=(t // tm, d // tn),
        in_specs=[
            pl.BlockSpec((tm, k), lambda i, j: (i, 0)),
            pl.BlockSpec((k, tn), lambda i, j: (0, j)),
            pl.BlockSpec((tm, tn), lambda i, j: (i, j)),
            pl.BlockSpec((None, 1, tn), lambda i, j: (lay.seq_of(i * tm), 0, j)),
        ],
        out_specs=pl.BlockSpec((tm, tn), lambda i, j: (i, j)),
        out_shape=jax.ShapeDtypeStruct((t, d), F32),
        compiler_params=_params(("parallel", "parallel")),
        name="out_proj",
    )(a, w_bf16, x, gt)


def _router_kernel(x_ref, g_ref, sc_ref, sh_ref, whi_ref, wlo_ref, b_ref, h_ref, gate_ref, idx_ref):
    h = _rms(x_ref[...], g_ref[...]) * (1.0 + sc_ref[...]) + sh_ref[...]
    h_hi = h.astype(BF16)
    h_lo = (h - h_hi.astype(F32)).astype(BF16)
    tm = x_ref.shape[0]
    nc = h_ref.shape[0] // tm
    for c in range(nc):
        h_ref[pl.ds(c, tm, stride=nc), :] = h[:, c * LANES:(c + 1) * LANES]
    logits = (jnp.dot(h_hi, whi_ref[...], preferred_element_type=F32)
              + (jnp.dot(h_hi, wlo_ref[...], preferred_element_type=F32)
                 + jnp.dot(h_lo, whi_ref[...], preferred_element_type=F32))
              + b_ref[...])
    lane = lax.broadcasted_iota(jnp.int32, logits.shape, 1)
    neg = jnp.float32(-jnp.inf)
    is_grp = lane < N_GROUPS
    gl = jnp.where(is_grp, logits, neg)
    gmax = jnp.max(gl, axis=-1, keepdims=True)
    grp = jnp.min(jnp.where(gl == gmax, lane, LANES), axis=-1, keepdims=True)
    p_grp = 1.0 / jnp.sum(jnp.where(is_grp, jnp.exp(gl - gmax), 0.0), axis=-1, keepdims=True)
    lo = N_GROUPS + grp * EXPERTS_PER_GROUP
    el = jnp.where((lane >= lo) & (lane < lo + EXPERTS_PER_GROUP), logits, neg)
    t1 = jnp.max(el, axis=-1, keepdims=True)
    i1 = jnp.min(jnp.where(el == t1, lane, LANES), axis=-1, keepdims=True)
    el2 = jnp.where(lane == i1, neg, el)
    t2 = jnp.max(el2, axis=-1, keepdims=True)
    i2 = jnp.min(jnp.where(el2 == t2, lane, LANES), axis=-1, keepdims=True)
    e = jnp.exp(t2 - t1)
    w1 = p_grp / (1.0 + e)
    w2 = p_grp * e / (1.0 + e)
    gate_ref[...] = jnp.where(lane == 0, w1, jnp.where(lane == 1, w2, 0.0))
    idx_ref[...] = jnp.where(lane == 0, i1 - N_GROUPS, jnp.where(lane == 1, i2 - N_GROUPS, 0))


def _router(x, g, sc, sh, w_hi, w_lo, b_row, lay):
    t, d = x.shape
    tm = 256
    seq = lambda i: (lay.seq_of(i * tm), 0, 0)
    const = lambda i: (0, 0)
    return pl.pallas_call(
        _router_kernel,
        grid=(t // tm,),
        in_specs=[
            pl.BlockSpec((tm, d), lambda i: (i, 0)),
            pl.BlockSpec((1, d), const),
            pl.BlockSpec((None, 1, d), seq),
            pl.BlockSpec((None, 1, d), seq),
            pl.BlockSpec((d, LANES), const),
            pl.BlockSpec((d, LANES), const),
            pl.BlockSpec((1, LANES), const),
        ],
        out_specs=[pl.BlockSpec((tm * (d // LANES), LANES), lambda i: (i, 0)),
                   pl.BlockSpec((tm, LANES), lambda i: (i, 0)),
                   pl.BlockSpec((tm, LANES), lambda i: (i, 0))],
        out_shape=[jax.ShapeDtypeStruct((t * (d // LANES), LANES), F32),
                   jax.ShapeDtypeStruct((t, LANES), F32),
                   jax.ShapeDtypeStruct((t, LANES), jnp.int32)],
        compiler_params=_params(("parallel",)),
        name="moe_router",
    )(x, g.reshape(1, d), sc, sh, w_hi, w_lo, b_row)


def _gather_kernel(nblk_ref, cur_ref, nxt_ref, h_ref, o_ref, buf_ref, sem):
    i = pl.program_id(0)
    n = pl.num_programs(0)
    rb = o_ref.shape[0]
    nc = h_ref.shape[1]

    def copy(slot, j, tok):
        dst = buf_ref.at[slot, pl.ds(pl.multiple_of(j * ROW_SLAB, 8), nc), :]
        return pltpu.make_async_copy(h_ref.at[tok], dst, sem.at[slot])

    def start_all(idx_ref, slot):
        def body(j, c):
            copy(slot, j, idx_ref[0, 0, j]).start()
            return c
        lax.fori_loop(0, rb, body, 0, unroll=8)

    @pl.when(i == 0)
    def _():
        start_all(cur_ref, 0)

    @pl.when(jnp.logical_and(i + 1 < n, i + 1 < nblk_ref[0]))
    def _():
        start_all(nxt_ref, (i + 1) % 2)

    @pl.when(i < nblk_ref[0])
    def _():
        slot = i % 2

        def drain(j, c):
            copy(slot, j, 0).wait()
            return c
        lax.fori_loop(0, rb, drain, 0, unroll=8)
        for c in range(nc):
            o_ref[:, c * LANES:(c + 1) * LANES] = (
                buf_ref[slot, pl.ds(c, rb, stride=ROW_SLAB), :].astype(o_ref.dtype))

    @pl.when(i >= nblk_ref[0])
    def _():
        o_ref[...] = jnp.zeros(o_ref.shape, o_ref.dtype)


def _gather_rows(h3, row_tok, n_used_blocks):
    n_rows = row_tok.shape[0]
    _, nc, _ = h3.shape
    rb = MOE_BLOCK
    nb = n_rows // rb
    tok_blocks = row_tok.reshape(nb, 1, rb)
    return pl.pallas_call(
        _gather_kernel,
        grid_spec=pltpu.PrefetchScalarGridSpec(
            num_scalar_prefetch=1,
            grid=(nb,),
            in_specs=[
                pl.BlockSpec((1, 1, rb), lambda i, nu: (i, 0, 0), memory_space=pltpu.SMEM),
                pl.BlockSpec((1, 1, rb), lambda i, nu: (jnp.minimum(i + 1, nb - 1), 0, 0),
                             memory_space=pltpu.SMEM),
                pl.BlockSpec(memory_space=pl.ANY),
            ],
            out_specs=pl.BlockSpec((rb, nc * LANES), lambda i, nu: (i, 0)),
            scratch_shapes=[pltpu.VMEM((2, rb * ROW_SLAB, LANES), h3.dtype),
                            pltpu.SemaphoreType.DMA((2,))],
        ),
        out_shape=jax.ShapeDtypeStruct((n_rows, nc * LANES), BF16),
        compiler_params=_params(("arbitrary",)),
        name="moe_gather",
    )(n_used_blocks, tok_blocks, tok_blocks, h3)


def _moe_up_kernel(be_ref, first_ref, nblk_ref, x_ref, wg_ref, wu_ref, o_ref, wgb_ref, wub_ref):
    blk = pl.program_id(1)

    @pl.when(blk < nblk_ref[0])
    def _():
        @pl.when(first_ref[blk] == 1)
        def _():
            wgb_ref[...] = wg_ref[...].astype(BF16)
            wub_ref[...] = wu_ref[...].astype(BF16)

        x = x_ref[...]
        g = jnp.dot(x, wgb_ref[...], preferred_element_type=F32)
        u = jnp.dot(x, wub_ref[...], preferred_element_type=F32)
        o_ref[...] = (g * jax.nn.sigmoid(g) * u).astype(o_ref.dtype)

    @pl.when(blk >= nblk_ref[0])
    def _():
        o_ref[...] = jnp.zeros(o_ref.shape, o_ref.dtype)


def _moe_down_kernel(be_ref, first_ref, nblk_ref, h_ref, wd_ref, o_ref, wdb_ref, slab_ref):
    blk = pl.program_id(1)

    @pl.when(blk < nblk_ref[0])
    def _():
        @pl.when(first_ref[blk] == 1)
        def _():
            wdb_ref[...] = wd_ref[...].astype(BF16)

        acc = jnp.dot(h_ref[...], wdb_ref[...], preferred_element_type=F32).astype(o_ref.dtype)
        rows, nc, _ = o_ref.shape
        for c in range(nc):
            slab_ref[pl.ds(c, rows, stride=nc), :] = acc[:, c * LANES:(c + 1) * LANES]
        o_ref[...] = slab_ref[...].reshape(o_ref.shape)

    @pl.when(blk >= nblk_ref[0])
    def _():
        o_ref[...] = jnp.zeros(o_ref.shape, o_ref.dtype)


def _moe_experts(xs, be, first, nblk, w_gate, w_up, w_down, layer):
    n_rows, d = xs.shape
    de = w_gate.shape[-1]
    blk = MOE_BLOCK
    nb = n_rows // blk
    tn1, tn2 = min(512, de), min(1024, d)
    rowblk = lambda b, nblk_ref: jnp.minimum(b, nblk_ref[0] - 1)
    hmid = pl.pallas_call(
        _moe_up_kernel,
        grid_spec=pltpu.PrefetchScalarGridSpec(
            num_scalar_prefetch=3,
            grid=(de // tn1, nb),
            in_specs=[
                pl.BlockSpec((blk, d), lambda n, b, be, fi, nu: (rowblk(b, nu), 0)),
                pl.BlockSpec((None, None, d, tn1), lambda n, b, be, fi, nu: (layer, be[b], 0, n)),
                pl.BlockSpec((None, None, d, tn1), lambda n, b, be, fi, nu: (layer, be[b], 0, n)),
            ],
            out_specs=pl.BlockSpec((blk, tn1), lambda n, b, be, fi, nu: (b, n)),
            scratch_shapes=[pltpu.VMEM((d, tn1), BF16), pltpu.VMEM((d, tn1), BF16)],
        ),
        out_shape=jax.ShapeDtypeStruct((n_rows, de), BF16),
        compiler_params=_params(("arbitrary", "arbitrary")),
        name="moe_up",
    )(be, first, nblk, xs, w_gate, w_up)
    return pl.pallas_call(
        _moe_down_kernel,
        grid_spec=pltpu.PrefetchScalarGridSpec(
            num_scalar_prefetch=3,
            grid=(d // tn2, nb),
            in_specs=[
                pl.BlockSpec((blk, de), lambda n, b, be, fi, nu: (rowblk(b, nu), 0)),
                pl.BlockSpec((None, None, de, tn2), lambda n, b, be, fi, nu: (layer, be[b], 0, n)),
            ],
            out_specs=pl.BlockSpec((blk, tn2 // LANES, LANES),
                                   lambda n, b, be, fi, nu: (b, n, 0)),
            scratch_shapes=[pltpu.VMEM((de, tn2), BF16), pltpu.VMEM((blk * (tn2 // LANES), LANES), F32)],
        ),
        out_shape=jax.ShapeDtypeStruct((n_rows, d // LANES, LANES), F32),
        compiler_params=_params(("arbitrary", "arbitrary")),
        name="moe_down",
    )(be, first, nblk, hmid, w_down)


def _combine_kernel(cur_ref, nxt_ref, y_ref, x_ref, gate_ref, gt_ref, gf_ref, o_ref, buf_ref, sem,
                    *, final_norm):
    i = pl.program_id(0)
    n = pl.num_programs(0)
    tm = x_ref.shape[0]
    rows = TOP_K * tm
    nc = y_ref.shape[1]

    def copy(slot, j, r):
        dst = buf_ref.at[slot, pl.ds(pl.multiple_of(j * ROW_SLAB, 8), nc), :]
        return pltpu.make_async_copy(y_ref.at[r], dst, sem.at[slot])

    def start_all(idx_ref, slot):
        def body(j, c):
            copy(slot, j, idx_ref[0, 0, j]).start()
            return c
        lax.fori_loop(0, rows, body, 0, unroll=8)

    @pl.when(i == 0)
    def _():
        start_all(cur_ref, 0)

    @pl.when(i + 1 < n)
    def _():
        start_all(nxt_ref, (i + 1) % 2)

    slot = i % 2

    def drain(j, c):
        copy(slot, j, 0).wait()
        return c
    lax.fori_loop(0, rows, drain, 0, unroll=8)

    gate = gate_ref[...]
    lane = lax.broadcasted_iota(jnp.int32, gate.shape, 1)
    w1 = jnp.sum(jnp.where(lane == 0, gate, 0.0), axis=-1, keepdims=True)
    w2 = jnp.sum(jnp.where(lane == 1, gate, 0.0), axis=-1, keepdims=True)
    moe = jnp.concatenate(
        [w1 * buf_ref[slot, pl.ds(c, tm, stride=ROW_SLAB), :]
         + w2 * buf_ref[slot, pl.ds(tm * ROW_SLAB + c, tm, stride=ROW_SLAB), :]
         for c in range(nc)], axis=-1)
    out = x_ref[...] + gt_ref[...] * moe
    if final_norm:
        out = _rms(out, gf_ref[...])
    o_ref[...] = out


def _combine(y, dest_blocks, x, gates, gt, g_final, lay, final_norm):
    t, d = x.shape
    tm = 128
    nb = t // tm
    kern = functools.partial(_combine_kernel, final_norm=final_norm)
    return pl.pallas_call(
        kern,
        grid=(nb,),
        in_specs=[
            pl.BlockSpec((1, 1, 2 * tm), lambda i: (i, 0, 0), memory_space=pltpu.SMEM),
            pl.BlockSpec((1, 1, 2 * tm), lambda i: (jnp.minimum(i + 1, nb - 1), 0, 0),
                         memory_space=pltpu.SMEM),
            pl.BlockSpec(memory_space=pl.ANY),
            pl.BlockSpec((tm, d), lambda i: (i, 0)),
            pl.BlockSpec((tm, LANES), lambda i: (i, 0)),
            pl.BlockSpec((None, 1, d), lambda i: (lay.seq_of(i * tm), 0, 0)),
            pl.BlockSpec((1, d), lambda i: (0, 0)),
        ],
        out_specs=pl.BlockSpec((tm, d), lambda i: (i, 0)),
        out_shape=jax.ShapeDtypeStruct((t, d), F32),
        scratch_shapes=[pltpu.VMEM((2, TOP_K * tm * ROW_SLAB, LANES), y.dtype),
                        pltpu.SemaphoreType.DMA((2,))],
        compiler_params=_params(("arbitrary",)),
        name="moe_combine",
    )(dest_blocks, dest_blocks, y, x, gates, gt, g_final.reshape(1, d))


def _t5_bucket(rel):
    half = NUM_BUCKETS // 2
    max_exact = half // 2
    n = jnp.abs(rel)
    large = max_exact + (jnp.log(jnp.maximum(n, max_exact).astype(F32) / max_exact)
                         / math.log(MAX_DISTANCE / max_exact) * (half - max_exact)).astype(jnp.int32)
    large = jnp.minimum(large, half - 1)
    return jnp.where(rel > 0, half, 0) + jnp.where(n < max_exact, n, large)


def _rope_tables(s):
    rows = s // GRID_W
    row = jnp.repeat(jnp.arange(rows, dtype=F32), GRID_W)
    col = jnp.tile(jnp.arange(GRID_W, dtype=F32), rows)
    inv_freq = jnp.exp(-math.log(ROPE_THETA) * jnp.arange(0, AXIS_DIM, 2, dtype=F32) / AXIS_DIM)
    ang_r = row[:, None] * inv_freq[None, :]
    ang_c = col[:, None] * inv_freq[None, :]
    cos = jnp.concatenate([jnp.cos(ang_r)] * 2 + [jnp.cos(ang_c)] * 2, axis=-1)
    sin = jnp.concatenate([-jnp.sin(ang_r), jnp.sin(ang_r), -jnp.sin(ang_c), jnp.sin(ang_c)], axis=-1)
    return cos, sin


def _toeplitz_rel(rows, cols):
    j = jnp.arange(rows + cols, dtype=jnp.int32)
    return jnp.where(j < cols, j, j - (rows + cols))


def _toeplitz(w, rows, cols):
    period = rows + cols
    flat = jnp.tile(w, (1,) * (w.ndim - 1) + (rows,))[..., :rows * (period - 1)]
    return flat.reshape(w.shape[:-1] + (rows, period - 1))[..., :cols]


def _dilated_tables(bias_tab, kv, dil):
    offs = (jnp.arange(2 * B_HALF + 1, dtype=jnp.int32) - B_HALF) * dil
    bias = bias_tab[_t5_bucket(offs)].astype(F32)
    tq = 128
    shift = jnp.array([-B_HALF, 0, -2 * B_HALF], jnp.int32)[:, None]
    rel = _toeplitz_rel(tq, 2 * tq)[None, :] + shift
    vals = jnp.where((jnp.abs(rel) <= B_HALF)[..., None],
                     bias[jnp.clip(rel + B_HALF, 0, 2 * B_HALF)], NEG_INF)
    tab = _toeplitz(jnp.moveaxis(vals, -1, 0), tq, 2 * tq)
    return jnp.swapaxes(tab.reshape(kv, 3, 3, tq, 2 * tq), 1, 2)


def _diff_tables(bias_tab, t):
    rel = _toeplitz_rel(t, t)[None, :] + jnp.array([-t, 0, t], jnp.int32)[:, None]
    near = _toeplitz(jnp.moveaxis(bias_tab[_t5_bucket(rel)].astype(F32), -1, 0), t, t)
    far = bias_tab[_t5_bucket(jnp.array([-2 * t, 2 * t], jnp.int32))].astype(F32).T
    tile = lambda c: jnp.broadcast_to(c[:, None, None, None], (c.shape[0], 1, t, t))
    return jnp.concatenate([tile(far[:, 0]), near, tile(far[:, 1])], axis=1) * LOG2E


def _routing(idx, blk):
    t = idx.shape[0]
    n_assign = t * TOP_K
    flat_e = idx.reshape(-1)
    onehot = (flat_e[:, None] == jnp.arange(N_EXPERTS, dtype=jnp.int32)[None, :]).astype(jnp.int32)
    csum = jnp.cumsum(onehot, axis=0)
    rank = jnp.take_along_axis(csum, flat_e[:, None], axis=1)[:, 0] - 1
    counts = csum[-1]
    padded = (counts + blk - 1) // blk * blk
    pad_end = jnp.cumsum(padded)
    pad_start = pad_end - padded
    dest = (pad_start[flat_e] + rank).astype(jnp.int32)
    n_rows = n_assign + N_EXPERTS * blk
    n_blocks = n_rows // blk
    flat_tok = jnp.arange(n_assign, dtype=jnp.int32) // TOP_K
    row_tok = jnp.zeros((n_rows,), jnp.int32).at[dest].set(flat_tok)
    n_used = (pad_end[-1] // blk).astype(jnp.int32)
    blk_ids = jnp.minimum(jnp.arange(n_blocks, dtype=jnp.int32), n_used - 1)
    be = jnp.minimum(jnp.searchsorted(pad_end, blk_ids * blk, side="right"), N_EXPERTS - 1).astype(jnp.int32)
    first = jnp.concatenate([jnp.ones((1,), jnp.int32), (be[1:] != be[:-1]).astype(jnp.int32)])
    return dest.reshape(t, TOP_K), row_tok, be, first, n_used.reshape(1)


def _dims(d):
    slots = d // HEAD_DIM
    a_heads = 3 * slots // 8
    a_kv = a_heads // 3
    b_heads = 3 * slots // 8
    b_kv = b_heads // 3
    c_heads = slots // 8
    c_w = c_heads * 2 * HEAD_DIM
    offs, acc = [], 0
    for n in (b_heads, b_kv, b_kv, a_heads, a_kv, a_kv):
        offs.append(acc)
        acc += n * HEAD_DIM
    for n in (c_w, c_w, c_w):
        offs.append(acc)
        acc += n
    names = ("qb_off", "kb_off", "vb_off", "qa_off", "ka_off", "va_off", "qc_off", "kc_off", "vc_off")
    out = dict(zip(names, offs))
    out.update(a_heads=a_heads, a_kv=a_kv, a_grp=3, b_heads=b_heads, b_kv=b_kv, c_heads=c_heads,
               a_w=a_heads * HEAD_DIM, b_w=b_heads * HEAD_DIM, c_w=c_w, in_w=acc)
    return out


def kernel(x_prompt, x_sample, c_prompt, c_sample, rel_bias, w_ada, b_ada, g_norm1, w_in, g_qk, lam_c,
           g_out, w_out, g_norm2, w_group, b_group, w_router, b_router, w_gate, w_up, w_down, g_final):
    bp, sp, d = x_prompt.shape
    bs, ss, _ = x_sample.shape
    depth = w_in.shape[0]
    lay = _Layout(bp, sp, bs, ss)
    dims = _dims(d)
    t = lay.t
    x = jnp.concatenate([x_prompt.reshape(bp * sp, d), x_sample.reshape(bs * ss, d)], axis=0)

    c_all = jnp.concatenate([c_prompt, c_sample], axis=0)
    pad = (-c_all.shape[0]) % 8
    c_pad = jnp.pad(c_all, ((0, pad), (0, 0)))
    mod = _ada_mod(c_pad, w_ada, b_ada)[:, :lay.nseq].reshape(depth, lay.nseq, 6, 1, d)

    rope_cos, rope_sin = _rope_tables(max(sp, ss))
    qk_w = dims["va_off"] - dims["qa_off"]
    b_tabs = [_dilated_tables(rel_bias[:, :dims["b_heads"]], dims["b_kv"], dil) for _, dil in B_BRANCHES]
    c_t = min(512, sp, ss)
    c_tab = _diff_tables(rel_bias[:, dims["b_heads"]:], c_t)

    for l in range(depth):
        lambda_init = 0.8 - 0.6 * math.exp(-0.3 * l)
        sh1, sc1, gt1, sh2, sc2, gt2 = (mod[l, :, i] for i in range(6))

        a_end = dims["a_w"] + 2 * dims["a_kv"] * HEAD_DIM
        b_end = a_end + dims["qa_off"]
        w_l = jnp.concatenate([w_in[l, :, a_end:b_end], w_in[l, :, :a_end], w_in[l, :, b_end:]],
                              axis=1).astype(BF16)
        p, *p_res = _inproj(x, g_norm1[l], sc1, sh1, w_l, lay, dims)
        p_by_dil = [p.reshape(1, t, p.shape[1])] + p_res
        g_row = jnp.concatenate([jnp.tile(g_qk[l, 0], dims["a_heads"]),
                                 jnp.tile(g_qk[l, 1], dims["a_kv"])]).reshape(1, qk_w)
        qk = _aprep(p, g_row, rope_cos, rope_sin, lay, dims["qa_off"], qk_w)

        lam = lam_c[l].astype(F32)
        lam_val = (jnp.exp(jnp.sum(lam[0] * lam[1])) - jnp.exp(jnp.sum(lam[2] * lam[3])) + lambda_init)
        lam_val = lam_val.reshape(1).astype(F32)
        g_c = g_out[l, dims["a_w"] + dims["b_w"]:].reshape(dims["c_heads"], 1, 2 * HEAD_DIM)

        oa = jnp.zeros((t, dims["a_w"]), F32)
        oc = jnp.zeros((t, dims["c_w"]), BF16)
        obs = [jnp.zeros((dil, t // dil, dims["b_w"]), F32) for _, dil in B_BRANCHES]
        lses = [jnp.zeros((dil, t // dil, dims["b_kv"] * LANES), F32) for _, dil in B_BRANCHES]
        for group in lay.groups:
            oa = _flash_a(qk, p, oa, group, dims)
            oc = _flash_c(p, lam_val, c_tab, g_c, oc, group, dims, 1.0 - lambda_init)
            for n, (_, dil) in enumerate(B_BRANCHES):
                obs[n], lses[n] = _dilated_branch(p_by_dil[n], b_tabs[n], obs[n], lses[n], group, dims, dil)
        mixed = _mix(oa, obs, lses, oc, g_out[l], dims)
        x = _outproj(mixed, w_out[l].astype(BF16), x, gt1, lay)

        w_r = jnp.concatenate([w_group[l], w_router[l],
                               jnp.zeros((d, LANES - N_GROUPS - N_EXPERTS), F32)], axis=1)
        w_hi = w_r.astype(BF16)
        w_lo = (w_r - w_hi.astype(F32)).astype(BF16)
        b_row = jnp.concatenate([b_group[l], b_router[l],
                                 jnp.zeros((LANES - N_GROUPS - N_EXPERTS,), F32)]).reshape(1, LANES)
        h2, gates, idx = _router(x, g_norm2[l], sc2, sh2, w_hi, w_lo, b_row.astype(F32), lay)
        dest, row_tok, be, first, n_used = _routing(idx[:, :TOP_K], MOE_BLOCK)
        xs = _gather_rows(h2.reshape(t, d // LANES, LANES), row_tok, n_used)
        y = _moe_experts(xs, be, first, n_used, w_gate, w_up, w_down, l)
        tmc = 128
        dest_blocks = dest.reshape(t // tmc, tmc, TOP_K).transpose(0, 2, 1).reshape(t // tmc, 1, TOP_K * tmc)
        x = _combine(y, dest_blocks, x, gates, gt2, g_final, lay, final_norm=(l == depth - 1))

    y_prompt = x[:lay.tp].reshape(bp, sp, d)
    y_sample = x[lay.tp:].reshape(bs, ss, d)
    return (y_prompt, y_sample)
```

```python
import functools
import math

import jax
import jax.numpy as jnp
from jax import lax
from jax.experimental import pallas as pl
from jax.experimental.pallas import tpu as pltpu

F32 = jnp.float32
BF16 = jnp.bfloat16

HEAD_DIM = 128
LANES = 128
GRID_W = 64
AXIS_DIM = HEAD_DIM // 2
ROPE_THETA = 10000.0
B_BRANCHES = ((128, 1), (512, 4), (2048, 16))
B_HALF = 64
NUM_BUCKETS = 32
MAX_DISTANCE = 128
N_GROUPS = 8
EXPERTS_PER_GROUP = 8
N_EXPERTS = N_GROUPS * EXPERTS_PER_GROUP
TOP_K = 2
NORM_EPS = 1e-6
NEG_INF = -1e30
LOG2E = math.log2(math.e)
VMEM_LIMIT_BYTES = 56 * 1024 * 1024
MOE_BLOCK = 256
ROW_SLAB = 40


def _params(sem):
    return pltpu.CompilerParams(dimension_semantics=sem, vmem_limit_bytes=VMEM_LIMIT_BYTES)


class _Layout:
    def __init__(self, bp, sp, bs, ss):
        self.bp, self.sp, self.bs, self.ss = bp, sp, bs, ss
        self.tp = bp * sp
        self.t = bp * sp + bs * ss
        self.nseq = bp + bs
        self.groups = ((0, bp, sp), (self.tp, bs, ss))

    def seq_of(self, row0):
        return jnp.where(row0 < self.tp, row0 // self.sp, self.bp + (row0 - self.tp) // self.ss)

    def pos_of(self, row0):
        return jnp.where(row0 < self.tp, row0 % self.sp, (row0 - self.tp) % self.ss)


def _rms(x, g):
    var = jnp.mean(x * x, axis=-1, keepdims=True)
    return x * lax.rsqrt(var + NORM_EPS) * g


def _ada_kernel(c_ref, w_ref, b_ref, o_ref):
    c = c_ref[...]
    cs = c * jax.nn.sigmoid(c)
    o_ref[...] = jnp.dot(cs, w_ref[...], preferred_element_type=F32) + b_ref[...]


def _ada_mod(c_all, w_ada, b_ada):
    depth, d, n = w_ada.shape
    rows = c_all.shape[0]
    tn = min(512, n)
    return pl.pallas_call(
        _ada_kernel,
        grid=(depth, n // tn),
        in_specs=[
            pl.BlockSpec((rows, d), lambda l, j: (0, 0)),
            pl.BlockSpec((None, d, tn), lambda l, j: (l, 0, j)),
            pl.BlockSpec((None, 1, tn), lambda l, j: (l, 0, j)),
        ],
        out_specs=pl.BlockSpec((None, rows, tn), lambda l, j: (l, 0, j)),
        out_shape=jax.ShapeDtypeStruct((depth, rows, n), F32),
        compiler_params=_params(("parallel", "parallel")),
        name="ada_mod",
    )(c_all, w_ada, b_ada.reshape(depth, 1, n))


def _inproj_kernel(x_ref, g_ref, sc_ref, sh_ref, w_ref, o_ref, *rest, dils, nb_tiles):
    res_refs, (h_ref, acc_ref) = rest[:len(dils)], rest[len(dils):]
    j = pl.program_id(1)
    tm = x_ref.shape[0]

    @pl.when(j == 0)
    def _():
        h = _rms(x_ref[...], g_ref[...]) * (1.0 + sc_ref[...]) + sh_ref[...]
        h_ref[...] = h.astype(BF16)

    acc = jnp.dot(h_ref[...], w_ref[...], preferred_element_type=F32)
    o_ref[...] = acc.astype(o_ref.dtype)

    @pl.when(j < nb_tiles)
    def _():
        for c in range(acc_ref.shape[0]):
            cols = slice(c * LANES, (c + 1) * LANES)
            acc_ref[c] = acc[:, cols]
            for dil, r_ref in zip(dils, res_refs):
                for r in range(dil):
                    r_ref[r, :, cols] = acc_ref[c, pl.ds(r, tm // dil, stride=dil), :].astype(r_ref.dtype)


def _inproj(x, g, sc, sh, w_bf16, lay, dims):
    t, d = x.shape
    n = w_bf16.shape[1]
    tm, tn = min(512, lay.sp), d // 8
    bw = dims["qa_off"]
    nb_tiles = bw // tn
    assert nb_tiles * tn == bw
    dils = tuple(dil for _, dil in B_BRANCHES if dil > 1)
    seq = lambda i, j: (lay.seq_of(i * tm), 0, 0)
    kern = functools.partial(_inproj_kernel, dils=dils, nb_tiles=nb_tiles)
    return pl.pallas_call(
        kern,
        grid=(t // tm, n // tn),
        in_specs=[
            pl.BlockSpec((tm, d), lambda i, j: (i, 0)),
            pl.BlockSpec((1, d), lambda i, j: (0, 0)),
            pl.BlockSpec((None, 1, d), seq),
            pl.BlockSpec((None, 1, d), seq),
            pl.BlockSpec((d, tn), lambda i, j: (0, j)),
        ],
        out_specs=[pl.BlockSpec((tm, tn), lambda i, j: (i, j))]
                  + [pl.BlockSpec((dil, tm // dil, tn), lambda i, j: (0, i, jnp.minimum(j, nb_tiles - 1)))
                     for dil in dils],
        out_shape=[jax.ShapeDtypeStruct((t, n), BF16)]
                  + [jax.ShapeDtypeStruct((dil, t // dil, bw), BF16) for dil in dils],
        scratch_shapes=[pltpu.VMEM((tm, d), BF16), pltpu.VMEM((tn // LANES, tm, LANES), F32)],
        compiler_params=_params(("parallel", "arbitrary")),
        name="in_proj",
    )(x, g.reshape(1, d), sc, sh, w_bf16)


def _aprep_kernel(p_ref, g_ref, cos_ref, sin_ref, o_ref):
    cos = cos_ref[...]
    sin = sin_ref[...]
    lane = lax.broadcasted_iota(jnp.int32, cos.shape, 1)
    first_half = (lane % AXIS_DIM) < (AXIS_DIM // 2)
    for h in range(p_ref.shape[1] // HEAD_DIM):
        sl = slice(h * HEAD_DIM, (h + 1) * HEAD_DIM)
        y = _rms(p_ref[:, sl].astype(F32), g_ref[:, sl])
        partner = jnp.where(first_half,
                            pltpu.roll(y, HEAD_DIM - AXIS_DIM // 2, 1),
                            pltpu.roll(y, AXIS_DIM // 2, 1))
        o_ref[:, sl] = (y * cos + partner * sin).astype(o_ref.dtype)


def _aprep(p, g_row, rope_cos, rope_sin, lay, col0, width):
    t = p.shape[0]
    tm, cw = min(512, lay.sp), width // 4
    assert col0 % cw == 0
    pos = lambda i, j: (lay.pos_of(i * tm) // tm, 0)
    return pl.pallas_call(
        _aprep_kernel,
        grid=(t // tm, width // cw),
        in_specs=[
            pl.BlockSpec((tm, cw), lambda i, j: (i, col0 // cw + j)),
            pl.BlockSpec((1, cw), lambda i, j: (0, j)),
            pl.BlockSpec((tm, HEAD_DIM), pos),
            pl.BlockSpec((tm, HEAD_DIM), pos),
        ],
        out_specs=pl.BlockSpec((tm, cw), lambda i, j: (i, j)),
        out_shape=jax.ShapeDtypeStruct((t, width), BF16),
        compiler_params=_params(("parallel", "parallel")),
        name="a_prep",
    )(p, g_row, rope_cos, rope_sin)


def _flash_a_kernel(q_ref, k_ref, v_ref, prev_ref, o_ref, qs_ref, *, scale, grp, tk, unroll):
    tq = q_ref.shape[0]
    rows = grp * tq
    for g in range(grp):
        qs_ref[g * tq:(g + 1) * tq, :] = q_ref[:, g * HEAD_DIM:(g + 1) * HEAD_DIM]
    q = qs_ref[...]
    c = scale * LOG2E
    reps = tk // LANES

    def body(j, carry):
        m, l, acc = carry
        off = pl.multiple_of(j * tk, tk)
        s = lax.dot_general(q, k_ref[pl.ds(off, tk), :], (((1,), (1,)), ((), ())),
                            preferred_element_type=F32)
        m_new = jnp.maximum(m, jnp.max(s, axis=-1, keepdims=True))
        alpha = jnp.exp2((m - m_new) * c)
        p = jnp.exp2((s - jnp.tile(m_new, (1, reps))) * c)
        ps = p[:, :LANES]
        for i in range(1, reps):
            ps = ps + p[:, i * LANES:(i + 1) * LANES]
        acc = alpha * acc + jnp.dot(p.astype(BF16), v_ref[pl.ds(off, tk), :],
                                    preferred_element_type=F32)
        return m_new, alpha * l + ps, acc

    init = (jnp.full((rows, LANES), -jnp.inf, F32), jnp.zeros((rows, LANES), F32),
            jnp.zeros((rows, HEAD_DIM), F32))
    _, l, acc = lax.fori_loop(0, k_ref.shape[0] // tk, body, init, unroll=unroll)
    out = acc / jnp.sum(l, axis=-1, keepdims=True)
    for g in range(grp):
        o_ref[:, g * HEAD_DIM:(g + 1) * HEAD_DIM] = out[g * tq:(g + 1) * tq].astype(o_ref.dtype)


def _flash_a(qk, p, oa, group, dims):
    base, nb, s = group
    grp = dims["a_grp"]
    kvh = dims["a_kv"]
    tq, tk = min(256, s), min(2048, s)
    nk = s // tk
    qw = grp * HEAD_DIM
    k_col0 = dims["a_heads"]
    v_col0 = dims["va_off"] // HEAD_DIM
    kern = functools.partial(_flash_a_kernel, scale=HEAD_DIM ** -0.5, grp=grp, tk=tk,
                             unroll=2 if nk % 2 == 0 else 1)
    out = pl.pallas_call(
        kern,
        grid=(nb, kvh, s // tq),
        in_specs=[
            pl.BlockSpec((tq, qw), lambda b, h, i: ((base + b * s) // tq + i, h)),
            pl.BlockSpec((s, HEAD_DIM), lambda b, h, i: (base // s + b, k_col0 + h)),
            pl.BlockSpec((s, HEAD_DIM), lambda b, h, i: (base // s + b, v_col0 + h)),
            pl.BlockSpec(memory_space=pl.ANY),
        ],
        out_specs=pl.BlockSpec((tq, qw), lambda b, h, i: ((base + b * s) // tq + i, h)),
        out_shape=jax.ShapeDtypeStruct(oa.shape, oa.dtype),
        scratch_shapes=[pltpu.VMEM((grp * tq, HEAD_DIM), BF16)],
        input_output_aliases={3: 0},
        compiler_params=_params(("parallel", "parallel", "arbitrary")),
        name="flash_a",
    )
    return out(qk, qk, p, oa)


def _dil_kernel(q_ref, k_ref, v_ref, tab_ref, prev_o_ref, prev_l_ref, o_ref, lse_ref, *, scale, tq):
    nsub = q_ref.shape[0] // tq
    nib = pl.num_programs(3) * nsub
    win = tab_ref.shape[-1]
    n_d = k_ref.shape[0]
    lane = lax.broadcasted_iota(jnp.int32, (tq, LANES), 1)
    for u in range(nsub):
        ib = pl.program_id(3) * nsub + u
        rows = slice(u * tq, (u + 1) * tq)
        ws = pl.multiple_of(jnp.clip(ib * tq - B_HALF, 0, n_d - win), B_HALF)
        variant = jnp.where(ib == 0, 1, jnp.where(ib == nib - 1, 2, 0))
        kw = k_ref[pl.ds(ws, win), :]
        vw = v_ref[pl.ds(ws, win), :]
        lse_tile = jnp.zeros((tq, LANES), F32)
        for g in range(3):
            cols = slice(g * HEAD_DIM, (g + 1) * HEAD_DIM)
            s = lax.dot_general(q_ref[rows, cols], kw, (((1,), (1,)), ((), ())),
                                preferred_element_type=F32) * scale + tab_ref[variant, g]
            m = jnp.max(s, axis=-1, keepdims=True)
            p = jnp.exp(s - m)
            l = jnp.sum(p, axis=-1, keepdims=True)
            o_ref[rows, cols] = jnp.dot(p.astype(BF16), vw, preferred_element_type=F32) / l
            lse_tile = jnp.where(lane == g, m + jnp.log(l), lse_tile)
        lse_ref[rows, :] = lse_tile


def _dilated_branch(pd, tab, ob, lse, group, dims, dil):
    base, nb, s = group
    kv = dims["b_kv"]
    n_d = s // dil
    tq = 128
    tb = tq * min(4, n_d // tq)
    assert n_d >= 2 * tq and base % s == 0 and tab.shape[-1] == 2 * tq and pd.shape[0] == dil
    q_cb = dims["qb_off"] // (3 * HEAD_DIM)
    k_cb, v_cb = (dims[n] // HEAD_DIM for n in ("kb_off", "vb_off"))
    row = lambda b, i: (base // dil + b * n_d) // tb + i
    kern = functools.partial(_dil_kernel, scale=HEAD_DIM ** -0.5, tq=tq)
    return pl.pallas_call(
        kern,
        grid=(nb, kv, dil, n_d // tb),
        in_specs=[
            pl.BlockSpec((None, tb, 3 * HEAD_DIM), lambda b, h, r, i: (r, row(b, i), q_cb + h)),
            pl.BlockSpec((None, n_d, HEAD_DIM), lambda b, h, r, i: (r, base // s + b, k_cb + h)),
            pl.BlockSpec((None, n_d, HEAD_DIM), lambda b, h, r, i: (r, base // s + b, v_cb + h)),
            pl.BlockSpec((None, 3, 3, tq, 2 * tq), lambda b, h, r, i: (h, 0, 0, 0, 0)),
            pl.BlockSpec(memory_space=pl.ANY),
            pl.BlockSpec(memory_space=pl.ANY),
        ],
        out_specs=[
            pl.BlockSpec((None, tb, 3 * HEAD_DIM), lambda b, h, r, i: (r, row(b, i), h)),
            pl.BlockSpec((None, tb, LANES), lambda b, h, r, i: (r, row(b, i), h)),
        ],
        out_shape=[jax.ShapeDtypeStruct(ob.shape, ob.dtype), jax.ShapeDtypeStruct(lse.shape, lse.dtype)],
        input_output_aliases={4: 0, 5: 1},
        compiler_params=_params(("parallel", "parallel", "parallel", "arbitrary")),
        name=f"dilated_{dil}",
    )(pd, pd, pd, tab, ob, lse)


def _flash_c_kernel(lam_ref, q_ref, k_ref, v_ref, tab_ref, g_ref, prev_ref, o_ref,
                    *, scale, out_scale, sub, unroll):
    qb = pl.program_id(2)
    t = q_ref.shape[0]
    c = scale * LOG2E
    reps = sub * t // LANES
    qs = (q_ref[:, :HEAD_DIM], q_ref[:, HEAD_DIM:])

    def body(j, carry):
        off = pl.multiple_of(j * (sub * t), sub * t)
        biases = [tab_ref[jnp.clip(j * sub + u - qb, -2, 2) + 2] for u in range(sub)]
        v = v_ref[pl.ds(off, sub * t), :]
        new = []
        for mi in range(2):
            m, l, acc = carry[mi]
            cols = slice(mi * HEAD_DIM, (mi + 1) * HEAD_DIM)
            s = jnp.concatenate(
                [lax.dot_general(qs[mi], k_ref[pl.ds(off + u * t, t), cols], (((1,), (1,)), ((), ())),
                                 preferred_element_type=F32) * c + biases[u] for u in range(sub)], axis=-1)
            m_new = jnp.maximum(m, jnp.max(s, axis=-1, keepdims=True))
            alpha = jnp.exp2(m - m_new)
            p = jnp.exp2(s - jnp.tile(m_new, (1, reps)))
            ps = p[:, :LANES]
            for i in range(1, reps):
                ps = ps + p[:, i * LANES:(i + 1) * LANES]
            acc = jnp.tile(alpha, (1, 2)) * acc + jnp.dot(p.astype(BF16), v,
                                                        preferred_element_type=F32)
            new.append((m_new, alpha * l + ps, acc))
        return tuple(new)

    one = (jnp.full((t, LANES), -jnp.inf, F32), jnp.zeros((t, LANES), F32),
           jnp.zeros((t, 2 * HEAD_DIM), F32))
    (_, l1, a1), (_, l2, a2) = lax.fori_loop(0, k_ref.shape[0] // (sub * t), body, (one, one),
                                             unroll=unroll)
    o = (a1 / jnp.sum(l1, axis=-1, keepdims=True)
         - lam_ref[0] * (a2 / jnp.sum(l2, axis=-1, keepdims=True)))
    o_ref[...] = (_rms(o, g_ref[...]) * out_scale).astype(o_ref.dtype)


def _flash_c(p, lam, tab, g_c, oc, group, dims, out_scale):
    base, nb, s = group
    heads = dims["c_heads"]
    t = tab.shape[-1]
    nk = s // t
    vw = 2 * HEAD_DIM
    q_cb, k_cb, v_cb = (dims[n] // vw for n in ("qc_off", "kc_off", "vc_off"))
    row = lambda b, i: (base + b * s) // t + i
    sub = 2 if nk % 2 == 0 else 1
    kern = functools.partial(_flash_c_kernel, scale=HEAD_DIM ** -0.5, out_scale=out_scale, sub=sub,
                             unroll=2 if (nk // sub) % 2 == 0 else 1)
    return pl.pallas_call(
        kern,
        grid=(nb, heads, nk),
        in_specs=[
            pl.BlockSpec(memory_space=pltpu.SMEM),
            pl.BlockSpec((t, vw), lambda b, h, i: (row(b, i), q_cb + h)),
            pl.BlockSpec((s, vw), lambda b, h, i: (base // s + b, k_cb + h)),
            pl.BlockSpec((s, vw), lambda b, h, i: (base // s + b, v_cb + h)),
            pl.BlockSpec((None, 5, t, t), lambda b, h, i: (h, 0, 0, 0)),
            pl.BlockSpec((None, 1, vw), lambda b, h, i: (h, 0, 0)),
            pl.BlockSpec(memory_space=pl.ANY),
        ],
        out_specs=pl.BlockSpec((t, vw), lambda b, h, i: (row(b, i), h)),
        out_shape=jax.ShapeDtypeStruct(oc.shape, oc.dtype),
        input_output_aliases={6: 0},
        compiler_params=_params(("parallel", "parallel", "arbitrary")),
        name="flash_c",
    )(lam, p, p, p, tab, g_c, oc)


def _mix_kernel(oa_ref, *rest, kv, nbr):
    src_o, src_l = rest[:nbr], rest[nbr:2 * nbr]
    oc_ref, g_ref, out_ref = rest[2 * nbr:2 * nbr + 3]
    tok_o, tok_l = rest[2 * nbr + 3:3 * nbr + 3], rest[3 * nbr + 3:]
    a_w = oa_ref.shape[1]
    b_w = src_o[0].shape[-1]
    tm = oa_ref.shape[0]
    for src, dst in zip(src_o + src_l, tok_o + tok_l):
        dil = src.shape[0]
        for c in range(dst.shape[0]):
            for r in range(dil):
                dst[c, pl.ds(r, tm // dil, stride=dil), :] = src[r, :, c * LANES:(c + 1) * LANES]
    g = g_ref[...]
    out_ref[:, :a_w] = _rms(oa_ref[...], g[:, :a_w]).astype(out_ref.dtype)
    lane = lax.broadcasted_iota(jnp.int32, (tm, LANES), 1)
    heads = []
    for h in range(kv):
        tiles = [r[h] for r in tok_l]
        for gg in range(3):
            ls = [jnp.sum(jnp.where(lane == gg, tl, 0.0), axis=-1, keepdims=True) for tl in tiles]
            mx = jnp.maximum(jnp.maximum(ls[0], ls[1]), ls[2])
            es = [jnp.exp(x - mx) for x in ls]
            den = es[0] + es[1] + es[2]
            heads.append(sum((e / den) * r[h * 3 + gg] for e, r in zip(es, tok_o)))
    ob = jnp.concatenate(heads, axis=-1)
    out_ref[:, a_w:a_w + b_w] = _rms(ob, g[:, a_w:a_w + b_w]).astype(out_ref.dtype)
    out_ref[:, a_w + b_w:] = oc_ref[...]


def _mix(oa, obs, lses, oc, g_out, dims):
    t = oa.shape[0]
    d = g_out.shape[-1]
    tm = 256
    full = lambda a: pl.BlockSpec((tm, a.shape[1]), lambda i: (i, 0))
    res = lambda a: pl.BlockSpec((a.shape[0], tm // a.shape[0], a.shape[2]), lambda i: (0, i, 0))
    return pl.pallas_call(
        functools.partial(_mix_kernel, kv=dims["b_kv"], nbr=len(obs)),
        grid=(t // tm,),
        in_specs=[full(oa)] + [res(a) for a in obs] + [res(a) for a in lses]
                 + [full(oc), pl.BlockSpec((1, d), lambda i: (0, 0))],
        out_specs=pl.BlockSpec((tm, d), lambda i: (i, 0)),
        out_shape=jax.ShapeDtypeStruct((t, d), BF16),
        scratch_shapes=[pltpu.VMEM((a.shape[2] // LANES, tm, LANES), F32) for a in obs + lses],
        compiler_params=_params(("parallel",)),
        name="mix_norm",
    )(oa, *obs, *lses, oc, g_out.reshape(1, d))


def _outproj_kernel(a_ref, w_ref, x_ref, gt_ref, o_ref):
    acc = jnp.dot(a_ref[...], w_ref[...], preferred_element_type=F32)
    o_ref[...] = x_ref[...] + gt_ref[...] * acc


def _outproj(a, w_bf16, x, gt, lay):
    t, d = x.shape
    k = a.shape[1]
    tm, tn = min(512, lay.sp), min(1024, d)
    return pl.pallas_call(
        _outproj_kernel,
        grid=(t // tm, d // tn),
        in_specs=[
            pl.BlockSpec((tm, k), lambda i, j: (i, 0)),
            pl.BlockSpec((k, tn), lambda i, j: (0, j)),
            pl.BlockSpec((tm, tn), lambda i, j: (i, j)),
            pl.BlockSpec((None, 1, tn), lambda i, j: (lay.seq_of(i * tm), 0, j)),
        ],
        out_specs=pl.BlockSpec((tm, tn), lambda i, j: (i, j)),
        out_shape=jax.ShapeDtypeStruct((t, d), F32),
        compiler_params=_params(("parallel", "parallel")),
        name="out_proj",
    )(a, w_bf16, x, gt)


def _router_kernel(x_ref, g_ref, sc_ref, sh_ref, whi_ref, wlo_ref, b_ref, h_ref, gate_ref, idx_ref):
    h = _rms(x_ref[...], g_ref[...]) * (1.0 + sc_ref[...]) + sh_ref[...]
    h_hi = h.astype(BF16)
    h_lo = (h - h_hi.astype(F32)).astype(BF16)
    tm = x_ref.shape[0]
    nc = h_ref.shape[0] // tm
    for c in range(nc):
        h_ref[pl.ds(c, tm, stride=nc), :] = h[:, c * LANES:(c + 1) * LANES]
    logits = (jnp.dot(h_hi, whi_ref[...], preferred_element_type=F32)
              + (jnp.dot(h_hi, wlo_ref[...], preferred_element_type=F32)
                 + jnp.dot(h_lo, whi_ref[...], preferred_element_type=F32))
              + b_ref[...])
    lane = lax.broadcasted_iota(jnp.int32, logits.shape, 1)
    neg = jnp.float32(-jnp.inf)
    is_grp = lane < N_GROUPS
    gl = jnp.where(is_grp, logits, neg)
    gmax = jnp.max(gl, axis=-1, keepdims=True)
    grp = jnp.min(jnp.where(gl == gmax, lane, LANES), axis=-1, keepdims=True)
    p_grp = 1.0 / jnp.sum(jnp.where(is_grp, jnp.exp(gl - gmax), 0.0), axis=-1, keepdims=True)
    lo = N_GROUPS + grp * EXPERTS_PER_GROUP
    el = jnp.where((lane >= lo) & (lane < lo + EXPERTS_PER_GROUP), logits, neg)
    t1 = jnp.max(el, axis=-1, keepdims=True)
    i1 = jnp.min(jnp.where(el == t1, lane, LANES), axis=-1, keepdims=True)
    el2 = jnp.where(lane == i1, neg, el)
    t2 = jnp.max(el2, axis=-1, keepdims=True)
    i2 = jnp.min(jnp.where(el2 == t2, lane, LANES), axis=-1, keepdims=True)
    e = jnp.exp(t2 - t1)
    w1 = p_grp / (1.0 + e)
    w2 = p_grp * e / (1.0 + e)
    gate_ref[...] = jnp.where(lane == 0, w1, jnp.where(lane == 1, w2, 0.0))
    idx_ref[...] = jnp.where(lane == 0, i1 - N_GROUPS, jnp.where(lane == 1, i2 - N_GROUPS, 0))


def _router(x, g, sc, sh, w_hi, w_lo, b_row, lay):
    t, d = x.shape
    tm = 256
    seq = lambda i: (lay.seq_of(i * tm), 0, 0)
    const = lambda i: (0, 0)
    return pl.pallas_call(
        _router_kernel,
        grid=(t // tm,),
        in_specs=[
            pl.BlockSpec((tm, d), lambda i: (i, 0)),
            pl.BlockSpec((1, d), const),
            pl.BlockSpec((None, 1, d), seq),
            pl.BlockSpec((None, 1, d), seq),
            pl.BlockSpec((d, LANES), const),
            pl.BlockSpec((d, LANES), const),
            pl.BlockSpec((1, LANES), const),
        ],
        out_specs=[pl.BlockSpec((tm * (d // LANES), LANES), lambda i: (i, 0)),
                   pl.BlockSpec((tm, LANES), lambda i: (i, 0)),
                   pl.BlockSpec((tm, LANES), lambda i: (i, 0))],
        out_shape=[jax.ShapeDtypeStruct((t * (d // LANES), LANES), F32),
                   jax.ShapeDtypeStruct((t, LANES), F32),
                   jax.ShapeDtypeStruct((t, LANES), jnp.int32)],
        compiler_params=_params(("parallel",)),
        name="moe_router",
    )(x, g.reshape(1, d), sc, sh, w_hi, w_lo, b_row)


def _gather_kernel(nblk_ref, cur_ref, nxt_ref, h_ref, o_ref, buf_ref, sem):
    i = pl.program_id(0)
    n = pl.num_programs(0)
    rb = o_ref.shape[0]
    nc = h_ref.shape[1]

    def copy(slot, j, tok):
        dst = buf_ref.at[slot, pl.ds(pl.multiple_of(j * ROW_SLAB, 8), nc), :]
        return pltpu.make_async_copy(h_ref.at[tok], dst, sem.at[slot])

    def start_all(idx_ref, slot):
        def body(j, c):
            copy(slot, j, idx_ref[0, 0, j]).start()
            return c
        lax.fori_loop(0, rb, body, 0, unroll=8)

    @pl.when(i == 0)
    def _():
        start_all(cur_ref, 0)

    @pl.when(jnp.logical_and(i + 1 < n, i + 1 < nblk_ref[0]))
    def _():
        start_all(nxt_ref, (i + 1) % 2)

    @pl.when(i < nblk_ref[0])
    def _():
        slot = i % 2

        def drain(j, c):
            copy(slot, j, 0).wait()
            return c
        lax.fori_loop(0, rb, drain, 0, unroll=8)
        for c in range(nc):
            o_ref[:, c * LANES:(c + 1) * LANES] = (
                buf_ref[slot, pl.ds(c, rb, stride=ROW_SLAB), :].astype(o_ref.dtype))

    @pl.when(i >= nblk_ref[0])
    def _():
        o_ref[...] = jnp.zeros(o_ref.shape, o_ref.dtype)


def _gather_rows(h3, row_tok, n_used_blocks):
    n_rows = row_tok.shape[0]
    _, nc, _ = h3.shape
    rb = MOE_BLOCK
    nb = n_rows // rb
    tok_blocks = row_tok.reshape(nb, 1, rb)
    return pl.pallas_call(
        _gather_kernel,
        grid_spec=pltpu.PrefetchScalarGridSpec(
            num_scalar_prefetch=1,
            grid=(nb,),
            in_specs=[
                pl.BlockSpec((1, 1, rb), lambda i, nu: (i, 0, 0), memory_space=pltpu.SMEM),
                pl.BlockSpec((1, 1, rb), lambda i, nu: (jnp.minimum(i + 1, nb - 1), 0, 0),
                             memory_space=pltpu.SMEM),
                pl.BlockSpec(memory_space=pl.ANY),
            ],
            out_specs=pl.BlockSpec((rb, nc * LANES), lambda i, nu: (i, 0)),
            scratch_shapes=[pltpu.VMEM((2, rb * ROW_SLAB, LANES), h3.dtype),
                            pltpu.SemaphoreType.DMA((2,))],
        ),
        out_shape=jax.ShapeDtypeStruct((n_rows, nc * LANES), BF16),
        compiler_params=_params(("arbitrary",)),
        name="moe_gather",
    )(n_used_blocks, tok_blocks, tok_blocks, h3)


def _expert_ring(bstart_ref, bcount_ref, total_ref, in_copy, out_copy, compute, zero_slot, n_blocks, n_pass):
    n = pl.program_id(0)
    e = pl.program_id(1)
    b0 = bstart_ref[e]
    nb = bcount_ref[e]
    total = total_ref[0]
    last_q = pl.num_programs(0) * total - 1

    @pl.when(jnp.logical_and(nb > 0, jnp.logical_and(n == 0, b0 == 0)))
    def _():
        in_copy(0, 0).start()

    def body(b, carry):
        gb = b0 + b
        q = n * total + gb
        slot = q % 2

        @pl.when(q < last_q)
        def _():
            in_copy(1 - slot, jnp.where(gb + 1 < total, gb + 1, 0)).start()

        in_copy(slot, 0).wait()

        @pl.when(q >= 2)
        def _():
            out_copy(slot, 0, 0).wait()

        compute(slot)
        out_copy(slot, gb, n).start()

        @pl.when(q == last_q)
        def _():
            @pl.when(q >= 1)
            def _():
                out_copy(1 - slot, 0, 0).wait()
            out_copy(slot, 0, 0).wait()

        return carry

    lax.fori_loop(0, nb, body, 0)

    @pl.when(jnp.logical_and(n == pl.num_programs(0) - 1, e == pl.num_programs(1) - 1))
    def _():
        zero_slot(0)

        def fill(gb, carry):
            for n_out in range(n_pass):
                cp = out_copy(0, gb, n_out)
                cp.start()
                cp.wait()
            return carry

        lax.fori_loop(total, n_blocks, fill, 0)


def _moe_up_kernel(bstart_ref, bcount_ref, total_ref, xs_ref, wg_ref, wu_ref, o_ref,
                   xbuf_ref, obuf_ref, wgb_ref, wub_ref, sem_in, sem_out):
    blk = xbuf_ref.shape[1]
    tn = obuf_ref.shape[2]

    @pl.when(bcount_ref[pl.program_id(1)] > 0)
    def _():
        wgb_ref[...] = wg_ref[...].astype(BF16)
        wub_ref[...] = wu_ref[...].astype(BF16)

    def in_copy(slot, gb):
        rows = pl.ds(pl.multiple_of(gb * blk, blk), blk)
        return pltpu.make_async_copy(xs_ref.at[rows], xbuf_ref.at[slot], sem_in.at[slot])

    def out_copy(slot, gb, n):
        rows = pl.ds(pl.multiple_of(gb * blk, blk), blk)
        cols = pl.ds(pl.multiple_of(n * tn, tn), tn)
        return pltpu.make_async_copy(obuf_ref.at[slot], o_ref.at[rows, cols], sem_out.at[slot])

    def compute(slot):
        x = xbuf_ref[slot]
        g = jnp.dot(x, wgb_ref[...], preferred_element_type=F32)
        u = jnp.dot(x, wub_ref[...], preferred_element_type=F32)
        obuf_ref[slot] = (g * jax.nn.sigmoid(g) * u).astype(obuf_ref.dtype)

    def zero_slot(slot):
        obuf_ref[slot] = jnp.zeros(obuf_ref.shape[1:], obuf_ref.dtype)

    _expert_ring(bstart_ref, bcount_ref, total_ref, in_copy, out_copy, compute, zero_slot,
                 o_ref.shape[0] // blk, o_ref.shape[1] // tn)


def _moe_down_kernel(bstart_ref, bcount_ref, total_ref, h_ref, wd_ref, o_ref,
                     hbuf_ref, obuf_ref, wdb_ref, slab_ref, sem_in, sem_out):
    blk = hbuf_ref.shape[1]
    nc = obuf_ref.shape[2]

    @pl.when(bcount_ref[pl.program_id(1)] > 0)
    def _():
        wdb_ref[...] = wd_ref[...].astype(BF16)

    def in_copy(slot, gb):
        rows = pl.ds(pl.multiple_of(gb * blk, blk), blk)
        return pltpu.make_async_copy(h_ref.at[rows], hbuf_ref.at[slot], sem_in.at[slot])

    def out_copy(slot, gb, n):
        rows = pl.ds(pl.multiple_of(gb * blk, blk), blk)
        slabs = pl.ds(pl.multiple_of(n * nc, nc), nc)
        return pltpu.make_async_copy(obuf_ref.at[slot], o_ref.at[rows, slabs], sem_out.at[slot])

    def compute(slot):
        acc = jnp.dot(hbuf_ref[slot], wdb_ref[...], preferred_element_type=F32)
        for c in range(nc):
            slab_ref[pl.ds(c, blk, stride=nc), :] = acc[:, c * LANES:(c + 1) * LANES]
        obuf_ref[slot] = slab_ref[...].reshape(blk, nc, LANES)

    def zero_slot(slot):
        obuf_ref[slot] = jnp.zeros(obuf_ref.shape[1:], obuf_ref.dtype)

    _expert_ring(bstart_ref, bcount_ref, total_ref, in_copy, out_copy, compute, zero_slot,
                 o_ref.shape[0] // blk, o_ref.shape[1] // nc)


def _moe_experts(xs, bstart, bcount, total, w_gate, w_up, w_down, layer):
    n_rows, d = xs.shape
    n_exp, de = w_gate.shape[1], w_gate.shape[-1]
    blk = MOE_BLOCK
    tn1, tn2 = min(512, de), min(2048, d)
    any_spec = pl.BlockSpec(memory_space=pl.ANY)
    wspec = lambda k, tn: pl.BlockSpec((None, None, k, tn), lambda n, e, bs, bc, tot: (layer, e, 0, n))
    dma2 = pltpu.SemaphoreType.DMA((2,))
    hmid = pl.pallas_call(
        _moe_up_kernel,
        grid_spec=pltpu.PrefetchScalarGridSpec(
            num_scalar_prefetch=3,
            grid=(de // tn1, n_exp),
            in_specs=[any_spec, wspec(d, tn1), wspec(d, tn1)],
            out_specs=any_spec,
            scratch_shapes=[pltpu.VMEM((2, blk, d), BF16), pltpu.VMEM((2, blk, tn1), BF16),
                            pltpu.VMEM((d, tn1), BF16), pltpu.VMEM((d, tn1), BF16), dma2, dma2],
        ),
        out_shape=jax.ShapeDtypeStruct((n_rows, de), BF16),
        compiler_params=_params(("arbitrary", "arbitrary")),
        name="moe_up",
    )(bstart, bcount, total, xs, w_gate, w_up)
    nc2 = tn2 // LANES
    return pl.pallas_call(
        _moe_down_kernel,
        grid_spec=pltpu.PrefetchScalarGridSpec(
            num_scalar_prefetch=3,
            grid=(d // tn2, n_exp),
            in_specs=[any_spec, wspec(de, tn2)],
            out_specs=any_spec,
            scratch_shapes=[pltpu.VMEM((2, blk, de), BF16), pltpu.VMEM((2, blk, nc2, LANES), F32),
                            pltpu.VMEM((de, tn2), BF16), pltpu.VMEM((blk * nc2, LANES), F32), dma2, dma2],
        ),
        out_shape=jax.ShapeDtypeStruct((n_rows, d // LANES, LANES), F32),
        compiler_params=_params(("arbitrary", "arbitrary")),
        name="moe_down",
    )(bstart, bcount, total, hmid, w_down)


def _combine_kernel(cur_ref, nxt_ref, y_ref, x_ref, gate_ref, gt_ref, gf_ref, *rest, final_norm, split):
    out_refs, (buf_ref, sem) = rest[:-2], rest[-2:]
    i = pl.program_id(0)
    n = pl.num_programs(0)
    tm = x_ref.shape[0]
    rows = TOP_K * tm
    nc = y_ref.shape[1]

    def copy(slot, j, r):
        dst = buf_ref.at[slot, pl.ds(pl.multiple_of(j * ROW_SLAB, 8), nc), :]
        return pltpu.make_async_copy(y_ref.at[r], dst, sem.at[slot])

    def start_all(idx_ref, slot):
        def body(j, c):
            copy(slot, j, idx_ref[0, 0, j]).start()
            return c
        lax.fori_loop(0, rows, body, 0, unroll=8)

    @pl.when(i == 0)
    def _():
        start_all(cur_ref, 0)

    @pl.when(i + 1 < n)
    def _():
        start_all(nxt_ref, (i + 1) % 2)

    slot = i % 2

    def drain(j, c):
        copy(slot, j, 0).wait()
        return c
    lax.fori_loop(0, rows, drain, 0, unroll=8)

    gate = gate_ref[...]
    lane = lax.broadcasted_iota(jnp.int32, gate.shape, 1)
    w1 = jnp.sum(jnp.where(lane == 0, gate, 0.0), axis=-1, keepdims=True)
    w2 = jnp.sum(jnp.where(lane == 1, gate, 0.0), axis=-1, keepdims=True)
    moe = jnp.concatenate(
        [w1 * buf_ref[slot, pl.ds(c, tm, stride=ROW_SLAB), :]
         + w2 * buf_ref[slot, pl.ds(tm * ROW_SLAB + c, tm, stride=ROW_SLAB), :]
         for c in range(nc)], axis=-1)
    out = x_ref[...] + gt_ref[...] * moe
    if final_norm:
        out = _rms(out, gf_ref[...])
    if split is None:
        out_refs[0][...] = out
    else:
        @pl.when(i < split)
        def _():
            out_refs[0][...] = out

        @pl.when(i >= split)
        def _():
            out_refs[1][...] = out


def _combine(y, dest_blocks, x, gates, gt, g_final, lay, final_norm, split_groups=False):
    t, d = x.shape
    tm = 128
    nb = t // tm
    split = lay.tp // tm if split_groups else None
    kern = functools.partial(_combine_kernel, final_norm=final_norm, split=split)
    if split_groups:
        out_specs = [pl.BlockSpec((tm, d), lambda i: (jnp.minimum(i, split - 1), 0)),
                     pl.BlockSpec((tm, d), lambda i: (jnp.maximum(i - split, 0), 0))]
        out_shape = [jax.ShapeDtypeStruct((lay.tp, d), F32), jax.ShapeDtypeStruct((t - lay.tp, d), F32)]
    else:
        out_specs = pl.BlockSpec((tm, d), lambda i: (i, 0))
        out_shape = jax.ShapeDtypeStruct((t, d), F32)
    return pl.pallas_call(
        kern,
        grid=(nb,),
        in_specs=[
            pl.BlockSpec((1, 1, 2 * tm), lambda i: (i, 0, 0), memory_space=pltpu.SMEM),
            pl.BlockSpec((1, 1, 2 * tm), lambda i: (jnp.minimum(i + 1, nb - 1), 0, 0),
                         memory_space=pltpu.SMEM),
            pl.BlockSpec(memory_space=pl.ANY),
            pl.BlockSpec((tm, d), lambda i: (i, 0)),
            pl.BlockSpec((tm, LANES), lambda i: (i, 0)),
            pl.BlockSpec((None, 1, d), lambda i: (lay.seq_of(i * tm), 0, 0)),
            pl.BlockSpec((1, d), lambda i: (0, 0)),
        ],
        out_specs=out_specs,
        out_shape=out_shape,
        scratch_shapes=[pltpu.VMEM((2, TOP_K * tm * ROW_SLAB, LANES), y.dtype),
                        pltpu.SemaphoreType.DMA((2,))],
        compiler_params=_params(("arbitrary",)),
        name="moe_combine",
    )(dest_blocks, dest_blocks, y, x, gates, gt, g_final.reshape(1, d))


def _t5_bucket(rel):
    half = NUM_BUCKETS // 2
    max_exact = half // 2
    n = jnp.abs(rel)
    large = max_exact + (jnp.log(jnp.maximum(n, max_exact).astype(F32) / max_exact)
                         / math.log(MAX_DISTANCE / max_exact) * (half - max_exact)).astype(jnp.int32)
    large = jnp.minimum(large, half - 1)
    return jnp.where(rel > 0, half, 0) + jnp.where(n < max_exact, n, large)


def _rope_tables(s):
    rows = s // GRID_W
    row = jnp.repeat(jnp.arange(rows, dtype=F32), GRID_W)
    col = jnp.tile(jnp.arange(GRID_W, dtype=F32), rows)
    inv_freq = jnp.exp(-math.log(ROPE_THETA) * jnp.arange(0, AXIS_DIM, 2, dtype=F32) / AXIS_DIM)
    ang_r = row[:, None] * inv_freq[None, :]
    ang_c = col[:, None] * inv_freq[None, :]
    cos = jnp.concatenate([jnp.cos(ang_r)] * 2 + [jnp.cos(ang_c)] * 2, axis=-1)
    sin = jnp.concatenate([-jnp.sin(ang_r), jnp.sin(ang_r), -jnp.sin(ang_c), jnp.sin(ang_c)], axis=-1)
    return cos, sin


def _toeplitz_rel(rows, cols):
    j = jnp.arange(rows + cols, dtype=jnp.int32)
    return jnp.where(j < cols, j, j - (rows + cols))


def _toeplitz(w, rows, cols):
    period = rows + cols
    flat = jnp.tile(w, (1,) * (w.ndim - 1) + (rows,))[..., :rows * (period - 1)]
    return flat.reshape(w.shape[:-1] + (rows, period - 1))[..., :cols]


def _dilated_tables(bias_tab, kv, dil):
    offs = (jnp.arange(2 * B_HALF + 1, dtype=jnp.int32) - B_HALF) * dil
    bias = bias_tab[_t5_bucket(offs)].astype(F32)
    tq = 128
    shift = jnp.array([-B_HALF, 0, -2 * B_HALF], jnp.int32)[:, None]
    rel = _toeplitz_rel(tq, 2 * tq)[None, :] + shift
    vals = jnp.where((jnp.abs(rel) <= B_HALF)[..., None],
                     bias[jnp.clip(rel + B_HALF, 0, 2 * B_HALF)], NEG_INF)
    tab = _toeplitz(jnp.moveaxis(vals, -1, 0), tq, 2 * tq)
    return jnp.swapaxes(tab.reshape(kv, 3, 3, tq, 2 * tq), 1, 2)


def _diff_tables(bias_tab, t):
    rel = _toeplitz_rel(t, t)[None, :] + jnp.array([-t, 0, t], jnp.int32)[:, None]
    near = _toeplitz(jnp.moveaxis(bias_tab[_t5_bucket(rel)].astype(F32), -1, 0), t, t)
    far = bias_tab[_t5_bucket(jnp.array([-2 * t, 2 * t], jnp.int32))].astype(F32).T
    tile = lambda c: jnp.broadcast_to(c[:, None, None, None], (c.shape[0], 1, t, t))
    return jnp.concatenate([tile(far[:, 0]), near, tile(far[:, 1])], axis=1) * LOG2E


def _routing(idx, blk):
    t = idx.shape[0]
    n_assign = t * TOP_K
    flat_e = idx.reshape(-1)
    onehot = (flat_e[:, None] == jnp.arange(N_EXPERTS, dtype=jnp.int32)[None, :]).astype(jnp.int32)
    csum = jnp.cumsum(onehot, axis=0)
    rank = jnp.take_along_axis(csum, flat_e[:, None], axis=1)[:, 0] - 1
    counts = csum[-1]
    padded = (counts + blk - 1) // blk * blk
    pad_end = jnp.cumsum(padded)
    pad_start = pad_end - padded
    dest = (pad_start[flat_e] + rank).astype(jnp.int32)
    n_rows = n_assign + N_EXPERTS * blk
    flat_tok = jnp.arange(n_assign, dtype=jnp.int32) // TOP_K
    row_tok = jnp.zeros((n_rows,), jnp.int32).at[dest].set(flat_tok)
    n_used = (pad_end[-1] // blk).astype(jnp.int32).reshape(1)
    bstart = (pad_start // blk).astype(jnp.int32)
    bcount = (padded // blk).astype(jnp.int32)
    return dest.reshape(t, TOP_K), row_tok, bstart, bcount, n_used


def _dims(d):
    slots = d // HEAD_DIM
    a_heads = 3 * slots // 8
    a_kv = a_heads // 3
    b_heads = 3 * slots // 8
    b_kv = b_heads // 3
    c_heads = slots // 8
    c_w = c_heads * 2 * HEAD_DIM
    offs, acc = [], 0
    for n in (b_heads, b_kv, b_kv, a_heads, a_kv, a_kv):
        offs.append(acc)
        acc += n * HEAD_DIM
    for n in (c_w, c_w, c_w):
        offs.append(acc)
        acc += n
    names = ("qb_off", "kb_off", "vb_off", "qa_off", "ka_off", "va_off", "qc_off", "kc_off", "vc_off")
    out = dict(zip(names, offs))
    out.update(a_heads=a_heads, a_kv=a_kv, a_grp=3, b_heads=b_heads, b_kv=b_kv, c_heads=c_heads,
               a_w=a_heads * HEAD_DIM, b_w=b_heads * HEAD_DIM, c_w=c_w, in_w=acc)
    return out


def kernel(x_prompt, x_sample, c_prompt, c_sample, rel_bias, w_ada, b_ada, g_norm1, w_in, g_qk, lam_c,
           g_out, w_out, g_norm2, w_group, b_group, w_router, b_router, w_gate, w_up, w_down, g_final):
    bp, sp, d = x_prompt.shape
    bs, ss, _ = x_sample.shape
    depth = w_in.shape[0]
    lay = _Layout(bp, sp, bs, ss)
    dims = _dims(d)
    t = lay.t
    x = jnp.concatenate([x_prompt.reshape(bp * sp, d), x_sample.reshape(bs * ss, d)], axis=0)

    c_all = jnp.concatenate([c_prompt, c_sample], axis=0)
    pad = (-c_all.shape[0]) % 8
    c_pad = jnp.pad(c_all, ((0, pad), (0, 0)))
    mod = _ada_mod(c_pad, w_ada, b_ada)[:, :lay.nseq].reshape(depth, lay.nseq, 6, 1, d)

    rope_cos, rope_sin = _rope_tables(max(sp, ss))
    qk_w = dims["va_off"] - dims["qa_off"]
    b_tabs = [_dilated_tables(rel_bias[:, :dims["b_heads"]], dims["b_kv"], dil) for _, dil in B_BRANCHES]
    c_t = min(512, sp, ss)
    c_tab = _diff_tables(rel_bias[:, dims["b_heads"]:], c_t)

    for l in range(depth):
        lambda_init = 0.8 - 0.6 * math.exp(-0.3 * l)
        sh1, sc1, gt1, sh2, sc2, gt2 = (mod[l, :, i] for i in range(6))

        a_end = dims["a_w"] + 2 * dims["a_kv"] * HEAD_DIM
        b_end = a_end + dims["qa_off"]
        w_l = jnp.concatenate([w_in[l, :, a_end:b_end], w_in[l, :, :a_end], w_in[l, :, b_end:]],
                              axis=1).astype(BF16)
        p, *p_res = _inproj(x, g_norm1[l], sc1, sh1, w_l, lay, dims)
        p_by_dil = [p.reshape(1, t, p.shape[1])] + p_res
        g_row = jnp.concatenate([jnp.tile(g_qk[l, 0], dims["a_heads"]),
                                 jnp.tile(g_qk[l, 1], dims["a_kv"])]).reshape(1, qk_w)
        qk = _aprep(p, g_row, rope_cos, rope_sin, lay, dims["qa_off"], qk_w)

        lam = lam_c[l].astype(F32)
        lam_val = (jnp.exp(jnp.sum(lam[0] * lam[1])) - jnp.exp(jnp.sum(lam[2] * lam[3])) + lambda_init)
        lam_val = lam_val.reshape(1).astype(F32)
        g_c = g_out[l, dims["a_w"] + dims["b_w"]:].reshape(dims["c_heads"], 1, 2 * HEAD_DIM)

        oa = jnp.zeros((t, dims["a_w"]), F32)
        oc = jnp.zeros((t, dims["c_w"]), BF16)
        obs = [jnp.zeros((dil, t // dil, dims["b_w"]), F32) for _, dil in B_BRANCHES]
        lses = [jnp.zeros((dil, t // dil, dims["b_kv"] * LANES), F32) for _, dil in B_BRANCHES]
        for group in lay.groups:
            oa = _flash_a(qk, p, oa, group, dims)
            oc = _flash_c(p, lam_val, c_tab, g_c, oc, group, dims, 1.0 - lambda_init)
            for n, (_, dil) in enumerate(B_BRANCHES):
                obs[n], lses[n] = _dilated_branch(p_by_dil[n], b_tabs[n], obs[n], lses[n], group, dims, dil)
        mixed = _mix(oa, obs, lses, oc, g_out[l], dims)
        x = _outproj(mixed, w_out[l].astype(BF16), x, gt1, lay)

        w_r = jnp.concatenate([w_group[l], w_router[l],
                               jnp.zeros((d, LANES - N_GROUPS - N_EXPERTS), F32)], axis=1)
        w_hi = w_r.astype(BF16)
        w_lo = (w_r - w_hi.astype(F32)).astype(BF16)
        b_row = jnp.concatenate([b_group[l], b_router[l],
                                 jnp.zeros((LANES - N_GROUPS - N_EXPERTS,), F32)]).reshape(1, LANES)
        h2, gates, idx = _router(x, g_norm2[l], sc2, sh2, w_hi, w_lo, b_row.astype(F32), lay)
        dest, row_tok, bstart, bcount, n_used = _routing(idx[:, :TOP_K], MOE_BLOCK)
        xs = _gather_rows(h2.reshape(t, d // LANES, LANES), row_tok, n_used)
        y = _moe_experts(xs, bstart, bcount, n_used, w_gate, w_up, w_down, l)
        tmc = 128
        dest_blocks = dest.reshape(t // tmc, tmc, TOP_K).transpose(0, 2, 1).reshape(t // tmc, 1, TOP_K * tmc)
        last = l == depth - 1
        x = _combine(y, dest_blocks, x, gates, gt2, g_final, lay, final_norm=last, split_groups=last)

    return (x[0].reshape(bp, sp, d), x[1].reshape(bs, ss, d))
```

```python
import functools
import math

import jax
import jax.numpy as jnp
from jax import lax
from jax.experimental import pallas as pl
from jax.experimental.pallas import tpu as pltpu

F32 = jnp.float32
BF16 = jnp.bfloat16

HEAD_DIM = 128
LANES = 128
GRID_W = 64
AXIS_DIM = HEAD_DIM // 2
ROPE_THETA = 10000.0
B_BRANCHES = ((128, 1), (512, 4), (2048, 16))
B_HALF = 64
NUM_BUCKETS = 32
MAX_DISTANCE = 128
N_GROUPS = 8
EXPERTS_PER_GROUP = 8
N_EXPERTS = N_GROUPS * EXPERTS_PER_GROUP
TOP_K = 2
NORM_EPS = 1e-6
NEG_INF = -1e30
LOG2E = math.log2(math.e)
VMEM_LIMIT_BYTES = 56 * 1024 * 1024
MOE_BLOCK = 256
ROW_SLAB = 40


def _params(sem):
    return pltpu.CompilerParams(dimension_semantics=sem, vmem_limit_bytes=VMEM_LIMIT_BYTES)


class _Layout:
    def __init__(self, bp, sp, bs, ss):
        self.bp, self.sp, self.bs, self.ss = bp, sp, bs, ss
        self.tp = bp * sp
        self.t = bp * sp + bs * ss
        self.nseq = bp + bs
        self.groups = ((0, bp, sp), (self.tp, bs, ss))

    def seq_of(self, row0):
        return jnp.where(row0 < self.tp, row0 // self.sp, self.bp + (row0 - self.tp) // self.ss)

    def pos_of(self, row0):
        return jnp.where(row0 < self.tp, row0 % self.sp, (row0 - self.tp) % self.ss)


def _rms(x, g):
    var = jnp.mean(x * x, axis=-1, keepdims=True)
    return x * lax.rsqrt(var + NORM_EPS) * g


def _ada_kernel(c_ref, w_ref, b_ref, o_ref):
    c = c_ref[...]
    cs = c * jax.nn.sigmoid(c)
    o_ref[...] = jnp.dot(cs, w_ref[...], preferred_element_type=F32) + b_ref[...]


def _ada_mod(c_all, w_ada, b_ada):
    depth, d, n = w_ada.shape
    rows = c_all.shape[0]
    tn = min(512, n)
    return pl.pallas_call(
        _ada_kernel,
        grid=(depth, n // tn),
        in_specs=[
            pl.BlockSpec((rows, d), lambda l, j: (0, 0)),
            pl.BlockSpec((None, d, tn), lambda l, j: (l, 0, j)),
            pl.BlockSpec((None, 1, tn), lambda l, j: (l, 0, j)),
        ],
        out_specs=pl.BlockSpec((None, rows, tn), lambda l, j: (l, 0, j)),
        out_shape=jax.ShapeDtypeStruct((depth, rows, n), F32),
        compiler_params=_params(("parallel", "parallel")),
        name="ada_mod",
    )(c_all, w_ada, b_ada.reshape(depth, 1, n))


def _normmod_kernel(x_ref, g_ref, sc_ref, sh_ref, o_ref):
    h = _rms(x_ref[...], g_ref[...]) * (1.0 + sc_ref[...]) + sh_ref[...]
    o_ref[...] = h.astype(o_ref.dtype)


def _normmod(x, g, sc, sh, lay):
    t, d = x.shape
    tm = min(512, lay.sp)
    seq = lambda i: (lay.seq_of(i * tm), 0, 0)
    return pl.pallas_call(
        _normmod_kernel,
        grid=(t // tm,),
        in_specs=[
            pl.BlockSpec((tm, d), lambda i: (i, 0)),
            pl.BlockSpec((1, d), lambda i: (0, 0)),
            pl.BlockSpec((None, 1, d), seq),
            pl.BlockSpec((None, 1, d), seq),
        ],
        out_specs=pl.BlockSpec((tm, d), lambda i: (i, 0)),
        out_shape=jax.ShapeDtypeStruct((t, d), BF16),
        compiler_params=_params(("parallel",)),
        name="norm_mod",
    )(x, g.reshape(1, d), sc, sh)


def _inproj_kernel(h_ref, w_ref, o_ref, *rest, dils, nb_tiles):
    res_refs, (acc_ref,) = rest[:len(dils)], rest[len(dils):]
    j = pl.program_id(1)
    tm = h_ref.shape[0]

    acc = jnp.dot(h_ref[...], w_ref[...], preferred_element_type=F32)
    o_ref[...] = acc.astype(o_ref.dtype)

    @pl.when(j < nb_tiles)
    def _():
        for c in range(acc_ref.shape[0]):
            cols = slice(c * LANES, (c + 1) * LANES)
            acc_ref[c] = acc[:, cols]
            for dil, r_ref in zip(dils, res_refs):
                for r in range(dil):
                    r_ref[r, :, cols] = acc_ref[c, pl.ds(r, tm // dil, stride=dil), :].astype(r_ref.dtype)


def _inproj(h, w_bf16, lay, dims):
    t, d = h.shape
    n = w_bf16.shape[1]
    tm, tn = min(1024, lay.sp), d // 8
    bw = dims["qa_off"]
    nb_tiles = bw // tn
    assert nb_tiles * tn == bw
    dils = tuple(dil for _, dil in B_BRANCHES if dil > 1)
    kern = functools.partial(_inproj_kernel, dils=dils, nb_tiles=nb_tiles)
    return pl.pallas_call(
        kern,
        grid=(t // tm, n // tn),
        in_specs=[
            pl.BlockSpec((tm, d), lambda i, j: (i, 0)),
            pl.BlockSpec((d, tn), lambda i, j: (0, j)),
        ],
        out_specs=[pl.BlockSpec((tm, tn), lambda i, j: (i, j))]
                  + [pl.BlockSpec((dil, tm // dil, tn), lambda i, j: (0, i, jnp.minimum(j, nb_tiles - 1)))
                     for dil in dils],
        out_shape=[jax.ShapeDtypeStruct((t, n), BF16)]
                  + [jax.ShapeDtypeStruct((dil, t // dil, bw), BF16) for dil in dils],
        scratch_shapes=[pltpu.VMEM((tn // LANES, tm, LANES), F32)],
        compiler_params=_params(("parallel", "arbitrary")),
        name="in_proj",
    )(h, w_bf16)


def _aprep_kernel(p_ref, g_ref, cos_ref, sin_ref, o_ref):
    cos = cos_ref[...]
    sin = sin_ref[...]
    lane = lax.broadcasted_iota(jnp.int32, cos.shape, 1)
    first_half = (lane % AXIS_DIM) < (AXIS_DIM // 2)
    for h in range(p_ref.shape[1] // HEAD_DIM):
        sl = slice(h * HEAD_DIM, (h + 1) * HEAD_DIM)
        y = _rms(p_ref[:, sl].astype(F32), g_ref[:, sl])
        partner = jnp.where(first_half,
                            pltpu.roll(y, HEAD_DIM - AXIS_DIM // 2, 1),
                            pltpu.roll(y, AXIS_DIM // 2, 1))
        o_ref[:, sl] = (y * cos + partner * sin).astype(o_ref.dtype)


def _aprep(p, g_row, rope_cos, rope_sin, lay, col0, width):
    t = p.shape[0]
    tm, cw = min(512, lay.sp), width // 4
    assert col0 % cw == 0
    pos = lambda i, j: (lay.pos_of(i * tm) // tm, 0)
    return pl.pallas_call(
        _aprep_kernel,
        grid=(t // tm, width // cw),
        in_specs=[
            pl.BlockSpec((tm, cw), lambda i, j: (i, col0 // cw + j)),
            pl.BlockSpec((1, cw), lambda i, j: (0, j)),
            pl.BlockSpec((tm, HEAD_DIM), pos),
            pl.BlockSpec((tm, HEAD_DIM), pos),
        ],
        out_specs=pl.BlockSpec((tm, cw), lambda i, j: (i, j)),
        out_shape=jax.ShapeDtypeStruct((t, width), BF16),
        compiler_params=_params(("parallel", "parallel")),
        name="a_prep",
    )(p, g_row, rope_cos, rope_sin)


def _flash_a_kernel(q_ref, k_ref, v_ref, prev_ref, o_ref, qs_ref, *, scale, grp, tk, unroll):
    tq = q_ref.shape[0]
    rows = grp * tq
    for g in range(grp):
        qs_ref[g * tq:(g + 1) * tq, :] = q_ref[:, g * HEAD_DIM:(g + 1) * HEAD_DIM]
    q = qs_ref[...]
    c = scale * LOG2E
    reps = tk // LANES

    def body(j, carry):
        m, l, acc = carry
        off = pl.multiple_of(j * tk, tk)
        s = lax.dot_general(q, k_ref[pl.ds(off, tk), :], (((1,), (1,)), ((), ())),
                            preferred_element_type=F32)
        m_new = jnp.maximum(m, jnp.max(s, axis=-1, keepdims=True))
        alpha = jnp.exp2((m - m_new) * c)
        p = jnp.exp2((s - jnp.tile(m_new, (1, reps))) * c)
        ps = p[:, :LANES]
        for i in range(1, reps):
            ps = ps + p[:, i * LANES:(i + 1) * LANES]
        acc = alpha * acc + jnp.dot(p.astype(BF16), v_ref[pl.ds(off, tk), :],
                                    preferred_element_type=F32)
        return m_new, alpha * l + ps, acc

    init = (jnp.full((rows, LANES), -jnp.inf, F32), jnp.zeros((rows, LANES), F32),
            jnp.zeros((rows, HEAD_DIM), F32))
    _, l, acc = lax.fori_loop(0, k_ref.shape[0] // tk, body, init, unroll=unroll)
    out = acc / jnp.sum(l, axis=-1, keepdims=True)
    for g in range(grp):
        o_ref[:, g * HEAD_DIM:(g + 1) * HEAD_DIM] = out[g * tq:(g + 1) * tq].astype(o_ref.dtype)


def _flash_a(qk, p, oa, group, dims):
    base, nb, s = group
    grp = dims["a_grp"]
    kvh = dims["a_kv"]
    tq, tk = min(256, s), min(2048, s)
    nk = s // tk
    qw = grp * HEAD_DIM
    k_col0 = dims["a_heads"]
    v_col0 = dims["va_off"] // HEAD_DIM
    kern = functools.partial(_flash_a_kernel, scale=HEAD_DIM ** -0.5, grp=grp, tk=tk,
                             unroll=2 if nk % 2 == 0 else 1)
    out = pl.pallas_call(
        kern,
        grid=(nb, kvh, s // tq),
        in_specs=[
            pl.BlockSpec((tq, qw), lambda b, h, i: ((base + b * s) // tq + i, h)),
            pl.BlockSpec((s, HEAD_DIM), lambda b, h, i: (base // s + b, k_col0 + h)),
            pl.BlockSpec((s, HEAD_DIM), lambda b, h, i: (base // s + b, v_col0 + h)),
            pl.BlockSpec(memory_space=pl.ANY),
        ],
        out_specs=pl.BlockSpec((tq, qw), lambda b, h, i: ((base + b * s) // tq + i, h)),
        out_shape=jax.ShapeDtypeStruct(oa.shape, oa.dtype),
        scratch_shapes=[pltpu.VMEM((grp * tq, HEAD_DIM), BF16)],
        input_output_aliases={3: 0},
        compiler_params=_params(("parallel", "parallel", "arbitrary")),
        name="flash_a",
    )
    return out(qk, qk, p, oa)


def _dil_kernel(q_ref, k_ref, v_ref, tab_ref, prev_o_ref, prev_l_ref, o_ref, lse_ref, *, scale, tq):
    nsub = q_ref.shape[0] // tq
    nib = pl.num_programs(3) * nsub
    win = tab_ref.shape[-1]
    n_d = k_ref.shape[0]
    lane = lax.broadcasted_iota(jnp.int32, (tq, LANES), 1)
    for u in range(nsub):
        ib = pl.program_id(3) * nsub + u
        rows = slice(u * tq, (u + 1) * tq)
        ws = pl.multiple_of(jnp.clip(ib * tq - B_HALF, 0, n_d - win), B_HALF)
        variant = jnp.where(ib == 0, 1, jnp.where(ib == nib - 1, 2, 0))
        kw = k_ref[pl.ds(ws, win), :]
        vw = v_ref[pl.ds(ws, win), :]
        lse_tile = jnp.zeros((tq, LANES), F32)
        for g in range(3):
            cols = slice(g * HEAD_DIM, (g + 1) * HEAD_DIM)
            s = lax.dot_general(q_ref[rows, cols], kw, (((1,), (1,)), ((), ())),
                                preferred_element_type=F32) * scale + tab_ref[variant, g]
            m = jnp.max(s, axis=-1, keepdims=True)
            p = jnp.exp(s - m)
            l = jnp.sum(p, axis=-1, keepdims=True)
            o_ref[rows, cols] = jnp.dot(p.astype(BF16), vw, preferred_element_type=F32) / l
            lse_tile = jnp.where(lane == g, m + jnp.log(l), lse_tile)
        lse_ref[rows, :] = lse_tile


def _dilated_branch(pd, tab, ob, lse, group, dims, dil):
    base, nb, s = group
    kv = dims["b_kv"]
    n_d = s // dil
    tq = 128
    tb = tq * min(4, n_d // tq)
    assert n_d >= 2 * tq and base % s == 0 and tab.shape[-1] == 2 * tq and pd.shape[0] == dil
    q_cb = dims["qb_off"] // (3 * HEAD_DIM)
    k_cb, v_cb = (dims[n] // HEAD_DIM for n in ("kb_off", "vb_off"))
    row = lambda b, i: (base // dil + b * n_d) // tb + i
    kern = functools.partial(_dil_kernel, scale=HEAD_DIM ** -0.5, tq=tq)
    return pl.pallas_call(
        kern,
        grid=(nb, kv, dil, n_d // tb),
        in_specs=[
            pl.BlockSpec((None, tb, 3 * HEAD_DIM), lambda b, h, r, i: (r, row(b, i), q_cb + h)),
            pl.BlockSpec((None, n_d, HEAD_DIM), lambda b, h, r, i: (r, base // s + b, k_cb + h)),
            pl.BlockSpec((None, n_d, HEAD_DIM), lambda b, h, r, i: (r, base // s + b, v_cb + h)),
            pl.BlockSpec((None, 3, 3, tq, 2 * tq), lambda b, h, r, i: (h, 0, 0, 0, 0)),
            pl.BlockSpec(memory_space=pl.ANY),
            pl.BlockSpec(memory_space=pl.ANY),
        ],
        out_specs=[
            pl.BlockSpec((None, tb, 3 * HEAD_DIM), lambda b, h, r, i: (r, row(b, i), h)),
            pl.BlockSpec((None, tb, LANES), lambda b, h, r, i: (r, row(b, i), h)),
        ],
        out_shape=[jax.ShapeDtypeStruct(ob.shape, ob.dtype), jax.ShapeDtypeStruct(lse.shape, lse.dtype)],
        input_output_aliases={4: 0, 5: 1},
        compiler_params=_params(("parallel", "parallel", "parallel", "arbitrary")),
        name=f"dilated_{dil}",
    )(pd, pd, pd, tab, ob, lse)


def _flash_c_kernel(lam_ref, q_ref, k_ref, v_ref, tab_ref, g_ref, prev_ref, o_ref,
                    *, scale, out_scale, sub, unroll):
    qb = pl.program_id(2)
    t = q_ref.shape[0]
    c = scale * LOG2E
    reps = sub * t // LANES
    qs = (q_ref[:, :HEAD_DIM], q_ref[:, HEAD_DIM:])

    def body(j, carry):
        off = pl.multiple_of(j * (sub * t), sub * t)
        biases = [tab_ref[jnp.clip(j * sub + u - qb, -2, 2) + 2] for u in range(sub)]
        v = v_ref[pl.ds(off, sub * t), :]
        new = []
        for mi in range(2):
            m, l, acc = carry[mi]
            cols = slice(mi * HEAD_DIM, (mi + 1) * HEAD_DIM)
            s = jnp.concatenate(
                [lax.dot_general(qs[mi], k_ref[pl.ds(off + u * t, t), cols], (((1,), (1,)), ((), ())),
                                 preferred_element_type=F32) * c + biases[u] for u in range(sub)], axis=-1)
            m_new = jnp.maximum(m, jnp.max(s, axis=-1, keepdims=True))
            alpha = jnp.exp2(m - m_new)
            p = jnp.exp2(s - jnp.tile(m_new, (1, reps)))
            ps = p[:, :LANES]
            for i in range(1, reps):
                ps = ps + p[:, i * LANES:(i + 1) * LANES]
            acc = jnp.tile(alpha, (1, 2)) * acc + jnp.dot(p.astype(BF16), v,
                                                        preferred_element_type=F32)
            new.append((m_new, alpha * l + ps, acc))
        return tuple(new)

    one = (jnp.full((t, LANES), -jnp.inf, F32), jnp.zeros((t, LANES), F32),
           jnp.zeros((t, 2 * HEAD_DIM), F32))
    (_, l1, a1), (_, l2, a2) = lax.fori_loop(0, k_ref.shape[0] // (sub * t), body, (one, one),
                                             unroll=unroll)
    o = (a1 / jnp.sum(l1, axis=-1, keepdims=True)
         - lam_ref[0] * (a2 / jnp.sum(l2, axis=-1, keepdims=True)))
    o_ref[...] = (_rms(o, g_ref[...]) * out_scale).astype(o_ref.dtype)


def _flash_c(p, lam, tab, g_c, oc, group, dims, out_scale):
    base, nb, s = group
    heads = dims["c_heads"]
    t = tab.shape[-1]
    nk = s // t
    vw = 2 * HEAD_DIM
    q_cb, k_cb, v_cb = (dims[n] // vw for n in ("qc_off", "kc_off", "vc_off"))
    row = lambda b, i: (base + b * s) // t + i
    sub = 2 if nk % 2 == 0 else 1
    kern = functools.partial(_flash_c_kernel, scale=HEAD_DIM ** -0.5, out_scale=out_scale, sub=sub,
                             unroll=2 if (nk // sub) % 2 == 0 else 1)
    return pl.pallas_call(
        kern,
        grid=(nb, heads, nk),
        in_specs=[
            pl.BlockSpec(memory_space=pltpu.SMEM),
            pl.BlockSpec((t, vw), lambda b, h, i: (row(b, i), q_cb + h)),
            pl.BlockSpec((s, vw), lambda b, h, i: (base // s + b, k_cb + h)),
            pl.BlockSpec((s, vw), lambda b, h, i: (base // s + b, v_cb + h)),
            pl.BlockSpec((None, 5, t, t), lambda b, h, i: (h, 0, 0, 0)),
            pl.BlockSpec((None, 1, vw), lambda b, h, i: (h, 0, 0)),
            pl.BlockSpec(memory_space=pl.ANY),
        ],
        out_specs=pl.BlockSpec((t, vw), lambda b, h, i: (row(b, i), h)),
        out_shape=jax.ShapeDtypeStruct(oc.shape, oc.dtype),
        input_output_aliases={6: 0},
        compiler_params=_params(("parallel", "parallel", "arbitrary")),
        name="flash_c",
    )(lam, p, p, p, tab, g_c, oc)


def _mix_kernel(oa_ref, *rest, kv, nbr):
    src_o, src_l = rest[:nbr], rest[nbr:2 * nbr]
    oc_ref, g_ref, out_ref = rest[2 * nbr:2 * nbr + 3]
    tok_o, tok_l = rest[2 * nbr + 3:3 * nbr + 3], rest[3 * nbr + 3:]
    a_w = oa_ref.shape[1]
    b_w = src_o[0].shape[-1]
    tm = oa_ref.shape[0]
    for src, dst in zip(src_o + src_l, tok_o + tok_l):
        dil = src.shape[0]
        for c in range(dst.shape[0]):
            for r in range(dil):
                dst[c, pl.ds(r, tm // dil, stride=dil), :] = src[r, :, c * LANES:(c + 1) * LANES]
    g = g_ref[...]
    out_ref[:, :a_w] = _rms(oa_ref[...], g[:, :a_w]).astype(out_ref.dtype)
    lane = lax.broadcasted_iota(jnp.int32, (tm, LANES), 1)
    heads = []
    for h in range(kv):
        tiles = [r[h] for r in tok_l]
        for gg in range(3):
            ls = [jnp.sum(jnp.where(lane == gg, tl, 0.0), axis=-1, keepdims=True) for tl in tiles]
            mx = jnp.maximum(jnp.maximum(ls[0], ls[1]), ls[2])
            es = [jnp.exp(x - mx) for x in ls]
            den = es[0] + es[1] + es[2]
            heads.append(sum((e / den) * r[h * 3 + gg] for e, r in zip(es, tok_o)))
    ob = jnp.concatenate(heads, axis=-1)
    out_ref[:, a_w:a_w + b_w] = _rms(ob, g[:, a_w:a_w + b_w]).astype(out_ref.dtype)
    out_ref[:, a_w + b_w:] = oc_ref[...]


def _mix(oa, obs, lses, oc, g_out, dims):
    t = oa.shape[0]
    d = g_out.shape[-1]
    tm = 256
    full = lambda a: pl.BlockSpec((tm, a.shape[1]), lambda i: (i, 0))
    res = lambda a: pl.BlockSpec((a.shape[0], tm // a.shape[0], a.shape[2]), lambda i: (0, i, 0))
    return pl.pallas_call(
        functools.partial(_mix_kernel, kv=dims["b_kv"], nbr=len(obs)),
        grid=(t // tm,),
        in_specs=[full(oa)] + [res(a) for a in obs] + [res(a) for a in lses]
                 + [full(oc), pl.BlockSpec((1, d), lambda i: (0, 0))],
        out_specs=pl.BlockSpec((tm, d), lambda i: (i, 0)),
        out_shape=jax.ShapeDtypeStruct((t, d), BF16),
        scratch_shapes=[pltpu.VMEM((a.shape[2] // LANES, tm, LANES), F32) for a in obs + lses],
        compiler_params=_params(("parallel",)),
        name="mix_norm",
    )(oa, *obs, *lses, oc, g_out.reshape(1, d))


def _outproj_kernel(a_ref, w_ref, x_ref, gt_ref, o_ref):
    acc = jnp.dot(a_ref[...], w_ref[...], preferred_element_type=F32)
    o_ref[...] = x_ref[...] + gt_ref[...] * acc


def _outproj(a, w_bf16, x, gt, lay):
    t, d = x.shape
    k = a.shape[1]
    tm, tn = min(512, lay.sp), min(1024, d)
    return pl.pallas_call(
        _outproj_kernel,
        grid=(t // tm, d // tn),
        in_specs=[
            pl.BlockSpec((tm, k), lambda i, j: (i, 0)),
            pl.BlockSpec((k, tn), lambda i, j: (0, j)),
            pl.BlockSpec((tm, tn), lambda i, j: (i, j)),
            pl.BlockSpec((None, 1, tn), lambda i, j: (lay.seq_of(i * tm), 0, j)),
        ],
        out_specs=pl.BlockSpec((tm, tn), lambda i, j: (i, j)),
        out_shape=jax.ShapeDtypeStruct((t, d), F32),
        compiler_params=_params(("parallel", "parallel")),
        name="out_proj",
    )(a, w_bf16, x, gt)


def _router_kernel(x_ref, g_ref, sc_ref, sh_ref, whi_ref, wlo_ref, b_ref, h_ref, gate_ref, idx_ref,
                   slab_ref):
    h = _rms(x_ref[...], g_ref[...]) * (1.0 + sc_ref[...]) + sh_ref[...]
    h_hi = h.astype(BF16)
    h_lo = (h - h_hi.astype(F32)).astype(BF16)
    tm = x_ref.shape[0]
    nc = h_ref.shape[0] // tm
    @pl.when(pl.program_id(0) == 0)
    def _():
        slab_ref[...] = jnp.zeros(slab_ref.shape, slab_ref.dtype)

    for c in range(nc):
        slab_ref[pl.ds(c, tm, stride=ROW_SLAB), :] = h[:, c * LANES:(c + 1) * LANES]
    h_ref[...] = slab_ref[...].reshape(tm, ROW_SLAB, LANES)[:, :nc, :].reshape(tm * nc, LANES)
    logits = (jnp.dot(h_hi, whi_ref[...], preferred_element_type=F32)
              + (jnp.dot(h_hi, wlo_ref[...], preferred_element_type=F32)
                 + jnp.dot(h_lo, whi_ref[...], preferred_element_type=F32))
              + b_ref[...])
    lane = lax.broadcasted_iota(jnp.int32, logits.shape, 1)
    neg = jnp.float32(-jnp.inf)
    is_grp = lane < N_GROUPS
    gl = jnp.where(is_grp, logits, neg)
    gmax = jnp.max(gl, axis=-1, keepdims=True)
    grp = jnp.min(jnp.where(gl == gmax, lane, LANES), axis=-1, keepdims=True)
    p_grp = 1.0 / jnp.sum(jnp.where(is_grp, jnp.exp(gl - gmax), 0.0), axis=-1, keepdims=True)
    lo = N_GROUPS + grp * EXPERTS_PER_GROUP
    el = jnp.where((lane >= lo) & (lane < lo + EXPERTS_PER_GROUP), logits, neg)
    t1 = jnp.max(el, axis=-1, keepdims=True)
    i1 = jnp.min(jnp.where(el == t1, lane, LANES), axis=-1, keepdims=True)
    el2 = jnp.where(lane == i1, neg, el)
    t2 = jnp.max(el2, axis=-1, keepdims=True)
    i2 = jnp.min(jnp.where(el2 == t2, lane, LANES), axis=-1, keepdims=True)
    e = jnp.exp(t2 - t1)
    w1 = p_grp / (1.0 + e)
    w2 = p_grp * e / (1.0 + e)
    gate_ref[...] = jnp.where(lane == 0, w1, jnp.where(lane == 1, w2, 0.0))
    idx_ref[...] = jnp.where(lane == 0, i1 - N_GROUPS, jnp.where(lane == 1, i2 - N_GROUPS, 0))


def _router(x, g, sc, sh, w_hi, w_lo, b_row, lay):
    t, d = x.shape
    tm = 256
    seq = lambda i: (lay.seq_of(i * tm), 0, 0)
    const = lambda i: (0, 0)
    return pl.pallas_call(
        _router_kernel,
        grid=(t // tm,),
        in_specs=[
            pl.BlockSpec((tm, d), lambda i: (i, 0)),
            pl.BlockSpec((1, d), const),
            pl.BlockSpec((None, 1, d), seq),
            pl.BlockSpec((None, 1, d), seq),
            pl.BlockSpec((d, LANES), const),
            pl.BlockSpec((d, LANES), const),
            pl.BlockSpec((1, LANES), const),
        ],
        out_specs=[pl.BlockSpec((tm * (d // LANES), LANES), lambda i: (i, 0)),
                   pl.BlockSpec((tm, LANES), lambda i: (i, 0)),
                   pl.BlockSpec((tm, LANES), lambda i: (i, 0))],
        out_shape=[jax.ShapeDtypeStruct((t * (d // LANES), LANES), F32),
                   jax.ShapeDtypeStruct((t, LANES), F32),
                   jax.ShapeDtypeStruct((t, LANES), jnp.int32)],
        scratch_shapes=[pltpu.VMEM((tm * ROW_SLAB, LANES), F32)],
        compiler_params=_params(("arbitrary",)),
        name="moe_router",
    )(x, g.reshape(1, d), sc, sh, w_hi, w_lo, b_row)


def _gather_kernel(nblk_ref, cur_ref, nxt_ref, h_ref, o_ref, buf_ref, sem):
    i = pl.program_id(0)
    n = pl.num_programs(0)
    rb = o_ref.shape[0]
    nc = h_ref.shape[1]

    def copy(slot, j, tok):
        dst = buf_ref.at[slot, pl.ds(pl.multiple_of(j * ROW_SLAB, 8), nc), :]
        return pltpu.make_async_copy(h_ref.at[tok], dst, sem.at[slot])

    def start_all(idx_ref, slot):
        def body(j, c):
            copy(slot, j, idx_ref[0, 0, j]).start()
            return c
        lax.fori_loop(0, rb, body, 0, unroll=8)

    @pl.when(i == 0)
    def _():
        start_all(cur_ref, 0)

    @pl.when(jnp.logical_and(i + 1 < n, i + 1 < nblk_ref[0]))
    def _():
        start_all(nxt_ref, (i + 1) % 2)

    @pl.when(i < nblk_ref[0])
    def _():
        slot = i % 2

        def drain(j, c):
            copy(slot, j, 0).wait()
            return c
        lax.fori_loop(0, rb, drain, 0, unroll=8)
        for c in range(nc):
            o_ref[:, c * LANES:(c + 1) * LANES] = (
                buf_ref[slot, pl.ds(c, rb, stride=ROW_SLAB), :].astype(o_ref.dtype))

    @pl.when(i >= nblk_ref[0])
    def _():
        o_ref[...] = jnp.zeros(o_ref.shape, o_ref.dtype)


def _gather_rows(h3, row_tok, n_used_blocks):
    n_rows = row_tok.shape[0]
    _, nc, _ = h3.shape
    rb = MOE_BLOCK
    nb = n_rows // rb
    tok_blocks = row_tok.reshape(nb, 1, rb)
    return pl.pallas_call(
        _gather_kernel,
        grid_spec=pltpu.PrefetchScalarGridSpec(
            num_scalar_prefetch=1,
            grid=(nb,),
            in_specs=[
                pl.BlockSpec((1, 1, rb), lambda i, nu: (i, 0, 0), memory_space=pltpu.SMEM),
                pl.BlockSpec((1, 1, rb), lambda i, nu: (jnp.minimum(i + 1, nb - 1), 0, 0),
                             memory_space=pltpu.SMEM),
                pl.BlockSpec(memory_space=pl.ANY),
            ],
            out_specs=pl.BlockSpec((rb, nc * LANES), lambda i, nu: (i, 0)),
            scratch_shapes=[pltpu.VMEM((2, rb * ROW_SLAB, LANES), h3.dtype),
                            pltpu.SemaphoreType.DMA((2,))],
        ),
        out_shape=jax.ShapeDtypeStruct((n_rows, nc * LANES), BF16),
        compiler_params=_params(("arbitrary",)),
        name="moe_gather",
    )(n_used_blocks, tok_blocks, tok_blocks, h3)


def _expert_ring(bstart_ref, bcount_ref, total_ref, in_copy, out_copy, compute, zero_slot, n_blocks, n_pass):
    n = pl.program_id(0)
    e = pl.program_id(1)
    b0 = bstart_ref[e]
    nb = bcount_ref[e]
    total = total_ref[0]
    last_q = pl.num_programs(0) * total - 1

    @pl.when(jnp.logical_and(nb > 0, jnp.logical_and(n == 0, b0 == 0)))
    def _():
        in_copy(0, 0).start()

    def body(b, carry):
        gb = b0 + b
        q = n * total + gb
        slot = q % 2

        @pl.when(q < last_q)
        def _():
            in_copy(1 - slot, jnp.where(gb + 1 < total, gb + 1, 0)).start()

        in_copy(slot, 0).wait()

        @pl.when(q >= 2)
        def _():
            out_copy(slot, 0, 0).wait()

        compute(slot)
        out_copy(slot, gb, n).start()

        @pl.when(q == last_q)
        def _():
            @pl.when(q >= 1)
            def _():
                out_copy(1 - slot, 0, 0).wait()
            out_copy(slot, 0, 0).wait()

        return carry

    lax.fori_loop(0, nb, body, 0)

    @pl.when(jnp.logical_and(n == pl.num_programs(0) - 1, e == pl.num_programs(1) - 1))
    def _():
        zero_slot(0)

        def fill(gb, carry):
            for n_out in range(n_pass):
                cp = out_copy(0, gb, n_out)
                cp.start()
                cp.wait()
            return carry

        lax.fori_loop(total, n_blocks, fill, 0)


def _moe_up_kernel(bstart_ref, bcount_ref, total_ref, xs_ref, wg_ref, wu_ref, o_ref,
                   xbuf_ref, obuf_ref, wgb_ref, wub_ref, sem_in, sem_out):
    blk = xbuf_ref.shape[1]
    tn = obuf_ref.shape[2]

    @pl.when(bcount_ref[pl.program_id(1)] > 0)
    def _():
        wgb_ref[...] = wg_ref[...].astype(BF16)
        wub_ref[...] = wu_ref[...].astype(BF16)

    def in_copy(slot, gb):
        rows = pl.ds(pl.multiple_of(gb * blk, blk), blk)
        return pltpu.make_async_copy(xs_ref.at[rows], xbuf_ref.at[slot], sem_in.at[slot])

    def out_copy(slot, gb, n):
        rows = pl.ds(pl.multiple_of(gb * blk, blk), blk)
        cols = pl.ds(pl.multiple_of(n * tn, tn), tn)
        return pltpu.make_async_copy(obuf_ref.at[slot], o_ref.at[rows, cols], sem_out.at[slot])

    def compute(slot):
        x = xbuf_ref[slot]
        g = jnp.dot(x, wgb_ref[...], preferred_element_type=F32)
        u = jnp.dot(x, wub_ref[...], preferred_element_type=F32)
        obuf_ref[slot] = (g * jax.nn.sigmoid(g) * u).astype(obuf_ref.dtype)

    def zero_slot(slot):
        obuf_ref[slot] = jnp.zeros(obuf_ref.shape[1:], obuf_ref.dtype)

    _expert_ring(bstart_ref, bcount_ref, total_ref, in_copy, out_copy, compute, zero_slot,
                 o_ref.shape[0] // blk, o_ref.shape[1] // tn)


def _moe_down_kernel(bstart_ref, bcount_ref, total_ref, h_ref, wd_ref, o_ref,
                     hbuf_ref, obuf_ref, wdb_ref, slab_ref, sem_in, sem_out):
    blk = hbuf_ref.shape[1]
    nc = obuf_ref.shape[2]

    @pl.when(bcount_ref[pl.program_id(1)] > 0)
    def _():
        wdb_ref[...] = wd_ref[...].astype(BF16)

    def in_copy(slot, gb):
        rows = pl.ds(pl.multiple_of(gb * blk, blk), blk)
        return pltpu.make_async_copy(h_ref.at[rows], hbuf_ref.at[slot], sem_in.at[slot])

    def out_copy(slot, gb, n):
        rows = pl.ds(pl.multiple_of(gb * blk, blk), blk)
        slabs = pl.ds(pl.multiple_of(n * nc, nc), nc)
        return pltpu.make_async_copy(obuf_ref.at[slot], o_ref.at[rows, slabs], sem_out.at[slot])

    stride = slab_ref.shape[0] // blk

    @pl.when(jnp.logical_and(pl.program_id(0) == 0, pl.program_id(1) == 0))
    def _():
        slab_ref[...] = jnp.zeros(slab_ref.shape, slab_ref.dtype)

    def compute(slot):
        acc = jnp.dot(hbuf_ref[slot], wdb_ref[...], preferred_element_type=F32)
        for c in range(nc):
            slab_ref[pl.ds(c, blk, stride=stride), :] = acc[:, c * LANES:(c + 1) * LANES]
        obuf_ref[slot] = slab_ref[...].reshape(blk, stride, LANES)[:, :nc, :]

    def zero_slot(slot):
        obuf_ref[slot] = jnp.zeros(obuf_ref.shape[1:], obuf_ref.dtype)

    _expert_ring(bstart_ref, bcount_ref, total_ref, in_copy, out_copy, compute, zero_slot,
                 o_ref.shape[0] // blk, o_ref.shape[1] // nc)


def _moe_experts(xs, bstart, bcount, total, w_gate, w_up, w_down, layer):
    n_rows, d = xs.shape
    n_exp, de = w_gate.shape[1], w_gate.shape[-1]
    blk = MOE_BLOCK
    tn1, tn2 = min(512, de), min(2048, d)
    any_spec = pl.BlockSpec(memory_space=pl.ANY)
    wspec = lambda k, tn: pl.BlockSpec((None, None, k, tn), lambda n, e, bs, bc, tot: (layer, e, 0, n))
    dma2 = pltpu.SemaphoreType.DMA((2,))
    hmid = pl.pallas_call(
        _moe_up_kernel,
        grid_spec=pltpu.PrefetchScalarGridSpec(
            num_scalar_prefetch=3,
            grid=(de // tn1, n_exp),
            in_specs=[any_spec, wspec(d, tn1), wspec(d, tn1)],
            out_specs=any_spec,
            scratch_shapes=[pltpu.VMEM((2, blk, d), BF16), pltpu.VMEM((2, blk, tn1), BF16),
                            pltpu.VMEM((d, tn1), BF16), pltpu.VMEM((d, tn1), BF16), dma2, dma2],
        ),
        out_shape=jax.ShapeDtypeStruct((n_rows, de), BF16),
        compiler_params=_params(("arbitrary", "arbitrary")),
        name="moe_up",
    )(bstart, bcount, total, xs, w_gate, w_up)
    nc2 = tn2 // LANES
    return pl.pallas_call(
        _moe_down_kernel,
        grid_spec=pltpu.PrefetchScalarGridSpec(
            num_scalar_prefetch=3,
            grid=(d // tn2, n_exp),
            in_specs=[any_spec, wspec(de, tn2)],
            out_specs=any_spec,
            scratch_shapes=[pltpu.VMEM((2, blk, de), BF16), pltpu.VMEM((2, blk, nc2, LANES), F32),
                            pltpu.VMEM((de, tn2), BF16), pltpu.VMEM((blk * (nc2 + 8), LANES), F32),
                            dma2, dma2],
        ),
        out_shape=jax.ShapeDtypeStruct((n_rows, d // LANES, LANES), F32),
        compiler_params=_params(("arbitrary", "arbitrary")),
        name="moe_down",
    )(bstart, bcount, total, hmid, w_down)


def _combine_kernel(cur_ref, nxt_ref, y_ref, x_ref, gate_ref, gt_ref, gf_ref, *rest, final_norm, split):
    out_refs, (buf_ref, sem) = rest[:-2], rest[-2:]
    i = pl.program_id(0)
    n = pl.num_programs(0)
    tm = x_ref.shape[0]
    rows = TOP_K * tm
    nc = y_ref.shape[1]

    def copy(slot, j, r):
        dst = buf_ref.at[slot, pl.ds(pl.multiple_of(j * ROW_SLAB, 8), nc), :]
        return pltpu.make_async_copy(y_ref.at[r], dst, sem.at[slot])

    def start_all(idx_ref, slot):
        def body(j, c):
            copy(slot, j, idx_ref[0, 0, j]).start()
            return c
        lax.fori_loop(0, rows, body, 0, unroll=8)

    @pl.when(i == 0)
    def _():
        start_all(cur_ref, 0)

    @pl.when(i + 1 < n)
    def _():
        start_all(nxt_ref, (i + 1) % 2)

    slot = i % 2

    def drain(j, c):
        copy(slot, j, 0).wait()
        return c
    lax.fori_loop(0, rows, drain, 0, unroll=8)

    gate = gate_ref[...]
    lane = lax.broadcasted_iota(jnp.int32, gate.shape, 1)
    w1 = jnp.sum(jnp.where(lane == 0, gate, 0.0), axis=-1, keepdims=True)
    w2 = jnp.sum(jnp.where(lane == 1, gate, 0.0), axis=-1, keepdims=True)
    moe = jnp.concatenate(
        [w1 * buf_ref[slot, pl.ds(c, tm, stride=ROW_SLAB), :]
         + w2 * buf_ref[slot, pl.ds(tm * ROW_SLAB + c, tm, stride=ROW_SLAB), :]
         for c in range(nc)], axis=-1)
    out = x_ref[...] + gt_ref[...] * moe
    if final_norm:
        out = _rms(out, gf_ref[...])
    if split is None:
        out_refs[0][...] = out
    else:
        @pl.when(i < split)
        def _():
            out_refs[0][...] = out

        @pl.when(i >= split)
        def _():
            out_refs[1][...] = out


def _combine(y, dest_blocks, x, gates, gt, g_final, lay, final_norm, split_groups=False):
    t, d = x.shape
    tm = 128
    nb = t // tm
    split = lay.tp // tm if split_groups else None
    kern = functools.partial(_combine_kernel, final_norm=final_norm, split=split)
    if split_groups:
        out_specs = [pl.BlockSpec((tm, d), lambda i: (jnp.minimum(i, split - 1), 0)),
                     pl.BlockSpec((tm, d), lambda i: (jnp.maximum(i - split, 0), 0))]
        out_shape = [jax.ShapeDtypeStruct((lay.tp, d), F32), jax.ShapeDtypeStruct((t - lay.tp, d), F32)]
    else:
        out_specs = pl.BlockSpec((tm, d), lambda i: (i, 0))
        out_shape = jax.ShapeDtypeStruct((t, d), F32)
    return pl.pallas_call(
        kern,
        grid=(nb,),
        in_specs=[
            pl.BlockSpec((1, 1, 2 * tm), lambda i: (i, 0, 0), memory_space=pltpu.SMEM),
            pl.BlockSpec((1, 1, 2 * tm), lambda i: (jnp.minimum(i + 1, nb - 1), 0, 0),
                         memory_space=pltpu.SMEM),
            pl.BlockSpec(memory_space=pl.ANY),
            pl.BlockSpec((tm, d), lambda i: (i, 0)),
            pl.BlockSpec((tm, LANES), lambda i: (i, 0)),
            pl.BlockSpec((None, 1, d), lambda i: (lay.seq_of(i * tm), 0, 0)),
            pl.BlockSpec((1, d), lambda i: (0, 0)),
        ],
        out_specs=out_specs,
        out_shape=out_shape,
        scratch_shapes=[pltpu.VMEM((2, TOP_K * tm * ROW_SLAB, LANES), y.dtype),
                        pltpu.SemaphoreType.DMA((2,))],
        compiler_params=_params(("arbitrary",)),
        name="moe_combine",
    )(dest_blocks, dest_blocks, y, x, gates, gt, g_final.reshape(1, d))


def _t5_bucket(rel):
    half = NUM_BUCKETS // 2
    max_exact = half // 2
    n = jnp.abs(rel)
    large = max_exact + (jnp.log(jnp.maximum(n, max_exact).astype(F32) / max_exact)
                         / math.log(MAX_DISTANCE / max_exact) * (half - max_exact)).astype(jnp.int32)
    large = jnp.minimum(large, half - 1)
    return jnp.where(rel > 0, half, 0) + jnp.where(n < max_exact, n, large)


def _rope_tables(s):
    rows = s // GRID_W
    row = jnp.repeat(jnp.arange(rows, dtype=F32), GRID_W)
    col = jnp.tile(jnp.arange(GRID_W, dtype=F32), rows)
    inv_freq = jnp.exp(-math.log(ROPE_THETA) * jnp.arange(0, AXIS_DIM, 2, dtype=F32) / AXIS_DIM)
    ang_r = row[:, None] * inv_freq[None, :]
    ang_c = col[:, None] * inv_freq[None, :]
    cos = jnp.concatenate([jnp.cos(ang_r)] * 2 + [jnp.cos(ang_c)] * 2, axis=-1)
    sin = jnp.concatenate([-jnp.sin(ang_r), jnp.sin(ang_r), -jnp.sin(ang_c), jnp.sin(ang_c)], axis=-1)
    return cos, sin


def _toeplitz_rel(rows, cols):
    j = jnp.arange(rows + cols, dtype=jnp.int32)
    return jnp.where(j < cols, j, j - (rows + cols))


def _toeplitz(w, rows, cols):
    period = rows + cols
    flat = jnp.tile(w, (1,) * (w.ndim - 1) + (rows,))[..., :rows * (period - 1)]
    return flat.reshape(w.shape[:-1] + (rows, period - 1))[..., :cols]


def _dilated_tables(bias_tab, kv, dil):
    offs = (jnp.arange(2 * B_HALF + 1, dtype=jnp.int32) - B_HALF) * dil
    bias = bias_tab[_t5_bucket(offs)].astype(F32)
    tq = 128
    shift = jnp.array([-B_HALF, 0, -2 * B_HALF], jnp.int32)[:, None]
    rel = _toeplitz_rel(tq, 2 * tq)[None, :] + shift
    vals = jnp.where((jnp.abs(rel) <= B_HALF)[..., None],
                     bias[jnp.clip(rel + B_HALF, 0, 2 * B_HALF)], NEG_INF)
    tab = _toeplitz(jnp.moveaxis(vals, -1, 0), tq, 2 * tq)
    return jnp.swapaxes(tab.reshape(kv, 3, 3, tq, 2 * tq), 1, 2)


def _diff_tables(bias_tab, t):
    rel = _toeplitz_rel(t, t)[None, :] + jnp.array([-t, 0, t], jnp.int32)[:, None]
    near = _toeplitz(jnp.moveaxis(bias_tab[_t5_bucket(rel)].astype(F32), -1, 0), t, t)
    far = bias_tab[_t5_bucket(jnp.array([-2 * t, 2 * t], jnp.int32))].astype(F32).T
    tile = lambda c: jnp.broadcast_to(c[:, None, None, None], (c.shape[0], 1, t, t))
    return jnp.concatenate([tile(far[:, 0]), near, tile(far[:, 1])], axis=1) * LOG2E


def _routing(idx, blk):
    t = idx.shape[0]
    n_assign = t * TOP_K
    flat_e = idx.reshape(-1)
    onehot = (flat_e[:, None] == jnp.arange(N_EXPERTS, dtype=jnp.int32)[None, :]).astype(jnp.int32)
    csum = jnp.cumsum(onehot, axis=0)
    rank = jnp.take_along_axis(csum, flat_e[:, None], axis=1)[:, 0] - 1
    counts = csum[-1]
    padded = (counts + blk - 1) // blk * blk
    pad_end = jnp.cumsum(padded)
    pad_start = pad_end - padded
    dest = (pad_start[flat_e] + rank).astype(jnp.int32)
    n_rows = n_assign + N_EXPERTS * blk
    flat_tok = jnp.arange(n_assign, dtype=jnp.int32) // TOP_K
    row_tok = jnp.zeros((n_rows,), jnp.int32).at[dest].set(flat_tok)
    n_used = (pad_end[-1] // blk).astype(jnp.int32).reshape(1)
    bstart = (pad_start // blk).astype(jnp.int32)
    bcount = (padded // blk).astype(jnp.int32)
    return dest.reshape(t, TOP_K), row_tok, bstart, bcount, n_used


def _dims(d):
    slots = d // HEAD_DIM
    a_heads = 3 * slots // 8
    a_kv = a_heads // 3
    b_heads = 3 * slots // 8
    b_kv = b_heads // 3
    c_heads = slots // 8
    c_w = c_heads * 2 * HEAD_DIM
    offs, acc = [], 0
    for n in (b_heads, b_kv, b_kv, a_heads, a_kv, a_kv):
        offs.append(acc)
        acc += n * HEAD_DIM
    for n in (c_w, c_w, c_w):
        offs.append(acc)
        acc += n
    names = ("qb_off", "kb_off", "vb_off", "qa_off", "ka_off", "va_off", "qc_off", "kc_off", "vc_off")
    out = dict(zip(names, offs))
    out.update(a_heads=a_heads, a_kv=a_kv, a_grp=3, b_heads=b_heads, b_kv=b_kv, c_heads=c_heads,
               a_w=a_heads * HEAD_DIM, b_w=b_heads * HEAD_DIM, c_w=c_w, in_w=acc)
    return out


def kernel(x_prompt, x_sample, c_prompt, c_sample, rel_bias, w_ada, b_ada, g_norm1, w_in, g_qk, lam_c,
           g_out, w_out, g_norm2, w_group, b_group, w_router, b_router, w_gate, w_up, w_down, g_final):
    bp, sp, d = x_prompt.shape
    bs, ss, _ = x_sample.shape
    depth = w_in.shape[0]
    lay = _Layout(bp, sp, bs, ss)
    dims = _dims(d)
    t = lay.t
    x = jnp.concatenate([x_prompt.reshape(bp * sp, d), x_sample.reshape(bs * ss, d)], axis=0)

    c_all = jnp.concatenate([c_prompt, c_sample], axis=0)
    pad = (-c_all.shape[0]) % 8
    c_pad = jnp.pad(c_all, ((0, pad), (0, 0)))
    mod = _ada_mod(c_pad, w_ada, b_ada)[:, :lay.nseq].reshape(depth, lay.nseq, 6, 1, d)

    rope_cos, rope_sin = _rope_tables(max(sp, ss))
    qk_w = dims["va_off"] - dims["qa_off"]
    b_tabs = [_dilated_tables(rel_bias[:, :dims["b_heads"]], dims["b_kv"], dil) for _, dil in B_BRANCHES]
    c_t = min(512, sp, ss)
    c_tab = _diff_tables(rel_bias[:, dims["b_heads"]:], c_t)

    for l in range(depth):
        lambda_init = 0.8 - 0.6 * math.exp(-0.3 * l)
        sh1, sc1, gt1, sh2, sc2, gt2 = (mod[l, :, i] for i in range(6))

        a_end = dims["a_w"] + 2 * dims["a_kv"] * HEAD_DIM
        b_end = a_end + dims["qa_off"]
        w_l = jnp.concatenate([w_in[l, :, a_end:b_end], w_in[l, :, :a_end], w_in[l, :, b_end:]],
                              axis=1).astype(BF16)
        p, *p_res = _inproj(_normmod(x, g_norm1[l], sc1, sh1, lay), w_l, lay, dims)
        p_by_dil = [p.reshape(1, t, p.shape[1])] + p_res
        g_row = jnp.concatenate([jnp.tile(g_qk[l, 0], dims["a_heads"]),
                                 jnp.tile(g_qk[l, 1], dims["a_kv"])]).reshape(1, qk_w)
        qk = _aprep(p, g_row, rope_cos, rope_sin, lay, dims["qa_off"], qk_w)

        lam = lam_c[l].astype(F32)
        lam_val = (jnp.exp(jnp.sum(lam[0] * lam[1])) - jnp.exp(jnp.sum(lam[2] * lam[3])) + lambda_init)
        lam_val = lam_val.reshape(1).astype(F32)
        g_c = g_out[l, dims["a_w"] + dims["b_w"]:].reshape(dims["c_heads"], 1, 2 * HEAD_DIM)

        if l == 0:
            oa = jnp.zeros((t, dims["a_w"]), F32)
            oc = jnp.zeros((t, dims["c_w"]), BF16)
            obs = [jnp.zeros((dil, t // dil, dims["b_w"]), F32) for _, dil in B_BRANCHES]
            lses = [jnp.zeros((dil, t // dil, dims["b_kv"] * LANES), F32) for _, dil in B_BRANCHES]
        for group in lay.groups:
            oa = _flash_a(qk, p, oa, group, dims)
            oc = _flash_c(p, lam_val, c_tab, g_c, oc, group, dims, 1.0 - lambda_init)
            for n, (_, dil) in enumerate(B_BRANCHES):
                obs[n], lses[n] = _dilated_branch(p_by_dil[n], b_tabs[n], obs[n], lses[n], group, dims, dil)
        mixed = _mix(oa, obs, lses, oc, g_out[l], dims)
        x = _outproj(mixed, w_out[l].astype(BF16), x, gt1, lay)

        w_r = jnp.concatenate([w_group[l], w_router[l],
                               jnp.zeros((d, LANES - N_GROUPS - N_EXPERTS), F32)], axis=1)
        w_hi = w_r.astype(BF16)
        w_lo = (w_r - w_hi.astype(F32)).astype(BF16)
        b_row = jnp.concatenate([b_group[l], b_router[l],
                                 jnp.zeros((LANES - N_GROUPS - N_EXPERTS,), F32)]).reshape(1, LANES)
        h2, gates, idx = _router(x, g_norm2[l], sc2, sh2, w_hi, w_lo, b_row.astype(F32), lay)
        dest, row_tok, bstart, bcount, n_used = _routing(idx[:, :TOP_K], MOE_BLOCK)
        xs = _gather_rows(h2.reshape(t, d // LANES, LANES), row_tok, n_used)
        y = _moe_experts(xs, bstart, bcount, n_used, w_gate, w_up, w_down, l)
        tmc = 128
        dest_blocks = dest.reshape(t // tmc, tmc, TOP_K).transpose(0, 2, 1).reshape(t // tmc, 1, TOP_K * tmc)
        last = l == depth - 1
        x = _combine(y, dest_blocks, x, gates, gt2, g_final, lay, final_norm=last, split_groups=last)

    return (x[0].reshape(bp, sp, d), x[1].reshape(bs, ss, d))
```

```python
import functools
import math

import jax
import jax.numpy as jnp
from jax import lax
from jax.experimental import pallas as pl
from jax.experimental.pallas import tpu as pltpu

F32 = jnp.float32
BF16 = jnp.bfloat16

HEAD_DIM = 128
LANES = 128
GRID_W = 64
AXIS_DIM = HEAD_DIM // 2
ROPE_THETA = 10000.0
B_BRANCHES = ((128, 1), (512, 4), (2048, 16))
B_HALF = 64
NUM_BUCKETS = 32
MAX_DISTANCE = 128
N_GROUPS = 8
EXPERTS_PER_GROUP = 8
N_EXPERTS = N_GROUPS * EXPERTS_PER_GROUP
TOP_K = 2
NORM_EPS = 1e-6
NEG_INF = -1e30
LOG2E = math.log2(math.e)
VMEM_LIMIT_BYTES = 56 * 1024 * 1024
MOE_BLOCK = 256
ROW_SLAB = 40


def _params(sem):
    return pltpu.CompilerParams(dimension_semantics=sem, vmem_limit_bytes=VMEM_LIMIT_BYTES)


class _Layout:
    def __init__(self, bp, sp, bs, ss):
        self.bp, self.sp, self.bs, self.ss = bp, sp, bs, ss
        self.tp = bp * sp
        self.t = bp * sp + bs * ss
        self.nseq = bp + bs
        self.groups = ((0, bp, sp), (self.tp, bs, ss))

    def seq_of(self, row0):
        return jnp.where(row0 < self.tp, row0 // self.sp, self.bp + (row0 - self.tp) // self.ss)

    def pos_of(self, row0):
        return jnp.where(row0 < self.tp, row0 % self.sp, (row0 - self.tp) % self.ss)


def _largest_unroll(trips, cap=8):
    u = cap
    while trips % u:
        u //= 2
    return u


def _rms(x, g):
    var = jnp.mean(x * x, axis=-1, keepdims=True)
    return x * lax.rsqrt(var + NORM_EPS) * g


def _ada_kernel(c_ref, w_ref, b_ref, o_ref):
    c = c_ref[...]
    cs = c * jax.nn.sigmoid(c)
    o_ref[...] = jnp.dot(cs, w_ref[...], preferred_element_type=F32) + b_ref[...]


def _ada_mod(c_all, w_ada, b_ada):
    depth, d, n = w_ada.shape
    rows = c_all.shape[0]
    tn = min(512, n)
    return pl.pallas_call(
        _ada_kernel,
        grid=(depth, n // tn),
        in_specs=[
            pl.BlockSpec((rows, d), lambda l, j: (0, 0)),
            pl.BlockSpec((None, d, tn), lambda l, j: (l, 0, j)),
            pl.BlockSpec((None, 1, tn), lambda l, j: (l, 0, j)),
        ],
        out_specs=pl.BlockSpec((None, rows, tn), lambda l, j: (l, 0, j)),
        out_shape=jax.ShapeDtypeStruct((depth, rows, n), F32),
        compiler_params=_params(("parallel", "parallel")),
        name="ada_mod",
    )(c_all, w_ada, b_ada.reshape(depth, 1, n))


def _normmod_kernel(x_ref, g_ref, sc_ref, sh_ref, o_ref):
    h = _rms(x_ref[...], g_ref[...]) * (1.0 + sc_ref[...]) + sh_ref[...]
    o_ref[...] = h.astype(o_ref.dtype)


def _normmod(x, g, sc, sh, lay):
    t, d = x.shape
    tm = min(512, lay.sp)
    seq = lambda i: (lay.seq_of(i * tm), 0, 0)
    return pl.pallas_call(
        _normmod_kernel,
        grid=(t // tm,),
        in_specs=[
            pl.BlockSpec((tm, d), lambda i: (i, 0)),
            pl.BlockSpec((1, d), lambda i: (0, 0)),
            pl.BlockSpec((None, 1, d), seq),
            pl.BlockSpec((None, 1, d), seq),
        ],
        out_specs=pl.BlockSpec((tm, d), lambda i: (i, 0)),
        out_shape=jax.ShapeDtypeStruct((t, d), BF16),
        compiler_params=_params(("parallel",)),
        name="norm_mod",
    )(x, g.reshape(1, d), sc, sh)


def _inproj_kernel(h_ref, w_ref, o_ref, *rest, dils, nb_tiles):
    res_refs, (acc_ref,) = rest[:len(dils)], rest[len(dils):]
    j = pl.program_id(1)
    tm = h_ref.shape[0]

    acc = jnp.dot(h_ref[...], w_ref[...], preferred_element_type=F32)
    o_ref[...] = acc.astype(o_ref.dtype)

    @pl.when(j < nb_tiles)
    def _():
        for c in range(acc_ref.shape[0]):
            cols = slice(c * LANES, (c + 1) * LANES)
            acc_ref[c] = acc[:, cols]
            for dil, r_ref in zip(dils, res_refs):
                for r in range(dil):
                    r_ref[r, :, cols] = acc_ref[c, pl.ds(r, tm // dil, stride=dil), :].astype(r_ref.dtype)


def _inproj(h, w_bf16, lay, dims):
    t, d = h.shape
    n = w_bf16.shape[1]
    tm, tn = min(1024, lay.sp), d // 8
    bw = dims["qa_off"]
    nb_tiles = bw // tn
    assert nb_tiles * tn == bw
    dils = tuple(dil for _, dil in B_BRANCHES if dil > 1)
    kern = functools.partial(_inproj_kernel, dils=dils, nb_tiles=nb_tiles)
    return pl.pallas_call(
        kern,
        grid=(t // tm, n // tn),
        in_specs=[
            pl.BlockSpec((tm, d), lambda i, j: (i, 0)),
            pl.BlockSpec((d, tn), lambda i, j: (0, j)),
        ],
        out_specs=[pl.BlockSpec((tm, tn), lambda i, j: (i, j))]
                  + [pl.BlockSpec((dil, tm // dil, tn), lambda i, j: (0, i, jnp.minimum(j, nb_tiles - 1)))
                     for dil in dils],
        out_shape=[jax.ShapeDtypeStruct((t, n), BF16)]
                  + [jax.ShapeDtypeStruct((dil, t // dil, bw), BF16) for dil in dils],
        scratch_shapes=[pltpu.VMEM((tn // LANES, tm, LANES), F32)],
        compiler_params=_params(("parallel", "arbitrary")),
        name="in_proj",
    )(h, w_bf16)


def _aprep_kernel(p_ref, g_ref, cos_ref, sin_ref, o_ref):
    cos = cos_ref[...]
    sin = sin_ref[...]
    lane = lax.broadcasted_iota(jnp.int32, cos.shape, 1)
    first_half = (lane % AXIS_DIM) < (AXIS_DIM // 2)
    for h in range(p_ref.shape[1] // HEAD_DIM):
        sl = slice(h * HEAD_DIM, (h + 1) * HEAD_DIM)
        y = _rms(p_ref[:, sl].astype(F32), g_ref[:, sl])
        partner = jnp.where(first_half,
                            pltpu.roll(y, HEAD_DIM - AXIS_DIM // 2, 1),
                            pltpu.roll(y, AXIS_DIM // 2, 1))
        o_ref[:, sl] = (y * cos + partner * sin).astype(o_ref.dtype)


def _aprep(p, g_row, rope_cos, rope_sin, lay, col0, width):
    t = p.shape[0]
    tm, cw = min(512, lay.sp), width // 4
    assert col0 % cw == 0
    pos = lambda i, j: (lay.pos_of(i * tm) // tm, 0)
    return pl.pallas_call(
        _aprep_kernel,
        grid=(t // tm, width // cw),
        in_specs=[
            pl.BlockSpec((tm, cw), lambda i, j: (i, col0 // cw + j)),
            pl.BlockSpec((1, cw), lambda i, j: (0, j)),
            pl.BlockSpec((tm, HEAD_DIM), pos),
            pl.BlockSpec((tm, HEAD_DIM), pos),
        ],
        out_specs=pl.BlockSpec((tm, cw), lambda i, j: (i, j)),
        out_shape=jax.ShapeDtypeStruct((t, width), BF16),
        compiler_params=_params(("parallel", "parallel")),
        name="a_prep",
    )(p, g_row, rope_cos, rope_sin)


def _flash_a_kernel(q_ref, k_ref, v_ref, prev_ref, o_ref, qs_ref, *, scale, grp, tk, unroll):
    tq = q_ref.shape[0]
    rows = grp * tq
    for g in range(grp):
        qs_ref[g * tq:(g + 1) * tq, :] = q_ref[:, g * HEAD_DIM:(g + 1) * HEAD_DIM]
    q = qs_ref[...]
    c = scale * LOG2E
    reps = tk // LANES

    def body(j, carry):
        m, l, acc = carry
        off = pl.multiple_of(j * tk, tk)
        s = lax.dot_general(q, k_ref[pl.ds(off, tk), :], (((1,), (1,)), ((), ())),
                            preferred_element_type=F32)
        m_new = jnp.maximum(m, jnp.max(s, axis=-1, keepdims=True))
        alpha = jnp.exp2((m - m_new) * c)
        p = jnp.exp2((s - jnp.tile(m_new, (1, reps))) * c)
        ps = p[:, :LANES]
        for i in range(1, reps):
            ps = ps + p[:, i * LANES:(i + 1) * LANES]
        acc = alpha * acc + jnp.dot(p.astype(BF16), v_ref[pl.ds(off, tk), :],
                                    preferred_element_type=F32)
        return m_new, alpha * l + ps, acc

    init = (jnp.full((rows, LANES), -jnp.inf, F32), jnp.zeros((rows, LANES), F32),
            jnp.zeros((rows, HEAD_DIM), F32))
    _, l, acc = lax.fori_loop(0, k_ref.shape[0] // tk, body, init, unroll=unroll)
    out = acc / jnp.sum(l, axis=-1, keepdims=True)
    for g in range(grp):
        o_ref[:, g * HEAD_DIM:(g + 1) * HEAD_DIM] = out[g * tq:(g + 1) * tq].astype(o_ref.dtype)


def _flash_a(qk, p, oa, group, dims):
    base, nb, s = group
    grp = dims["a_grp"]
    kvh = dims["a_kv"]
    tq, tk = min(256, s), min(1024, s)
    nk = s // tk
    qw = grp * HEAD_DIM
    k_col0 = dims["a_heads"]
    v_col0 = dims["va_off"] // HEAD_DIM
    kern = functools.partial(_flash_a_kernel, scale=HEAD_DIM ** -0.5, grp=grp, tk=tk,
                             unroll=_largest_unroll(nk))
    out = pl.pallas_call(
        kern,
        grid=(nb, kvh, s // tq),
        in_specs=[
            pl.BlockSpec((tq, qw), lambda b, h, i: ((base + b * s) // tq + i, h)),
            pl.BlockSpec((s, HEAD_DIM), lambda b, h, i: (base // s + b, k_col0 + h)),
            pl.BlockSpec((s, HEAD_DIM), lambda b, h, i: (base // s + b, v_col0 + h)),
            pl.BlockSpec(memory_space=pl.ANY),
        ],
        out_specs=pl.BlockSpec((tq, qw), lambda b, h, i: ((base + b * s) // tq + i, h)),
        out_shape=jax.ShapeDtypeStruct(oa.shape, oa.dtype),
        scratch_shapes=[pltpu.VMEM((grp * tq, HEAD_DIM), BF16)],
        input_output_aliases={3: 0},
        compiler_params=_params(("parallel", "parallel", "arbitrary")),
        name="flash_a",
    )
    return out(qk, qk, p, oa)


def _dil_kernel(q_ref, k_ref, v_ref, tab_ref, prev_o_ref, prev_l_ref, o_ref, lse_ref, *, scale, tq):
    nsub = q_ref.shape[0] // tq
    nib = pl.num_programs(3) * nsub
    win = tab_ref.shape[-1]
    n_d = k_ref.shape[0]
    lane = lax.broadcasted_iota(jnp.int32, (tq, LANES), 1)
    for u in range(nsub):
        ib = pl.program_id(3) * nsub + u
        rows = slice(u * tq, (u + 1) * tq)
        ws = pl.multiple_of(jnp.clip(ib * tq - B_HALF, 0, n_d - win), B_HALF)
        variant = jnp.where(ib == 0, 1, jnp.where(ib == nib - 1, 2, 0))
        kw = k_ref[pl.ds(ws, win), :]
        vw = v_ref[pl.ds(ws, win), :]
        lse_tile = jnp.zeros((tq, LANES), F32)
        for g in range(3):
            cols = slice(g * HEAD_DIM, (g + 1) * HEAD_DIM)
            s = lax.dot_general(q_ref[rows, cols], kw, (((1,), (1,)), ((), ())),
                                preferred_element_type=F32) * scale + tab_ref[variant, g]
            m = jnp.max(s, axis=-1, keepdims=True)
            p = jnp.exp(s - m)
            l = jnp.sum(p, axis=-1, keepdims=True)
            o_ref[rows, cols] = jnp.dot(p.astype(BF16), vw, preferred_element_type=F32) / l
            lse_tile = jnp.where(lane == g, m + jnp.log(l), lse_tile)
        lse_ref[rows, :] = lse_tile


def _dilated_branch(pd, tab, ob, lse, group, dims, dil):
    base, nb, s = group
    kv = dims["b_kv"]
    n_d = s // dil
    tq = 128
    tb = tq * min(4, n_d // tq)
    assert n_d >= 2 * tq and base % s == 0 and tab.shape[-1] == 2 * tq and pd.shape[0] == dil
    q_cb = dims["qb_off"] // (3 * HEAD_DIM)
    k_cb, v_cb = (dims[n] // HEAD_DIM for n in ("kb_off", "vb_off"))
    row = lambda b, i: (base // dil + b * n_d) // tb + i
    kern = functools.partial(_dil_kernel, scale=HEAD_DIM ** -0.5, tq=tq)
    return pl.pallas_call(
        kern,
        grid=(nb, kv, dil, n_d // tb),
        in_specs=[
            pl.BlockSpec((None, tb, 3 * HEAD_DIM), lambda b, h, r, i: (r, row(b, i), q_cb + h)),
            pl.BlockSpec((None, n_d, HEAD_DIM), lambda b, h, r, i: (r, base // s + b, k_cb + h)),
            pl.BlockSpec((None, n_d, HEAD_DIM), lambda b, h, r, i: (r, base // s + b, v_cb + h)),
            pl.BlockSpec((None, 3, 3, tq, 2 * tq), lambda b, h, r, i: (h, 0, 0, 0, 0)),
            pl.BlockSpec(memory_space=pl.ANY),
            pl.BlockSpec(memory_space=pl.ANY),
        ],
        out_specs=[
            pl.BlockSpec((None, tb, 3 * HEAD_DIM), lambda b, h, r, i: (r, row(b, i), h)),
            pl.BlockSpec((None, tb, LANES), lambda b, h, r, i: (r, row(b, i), h)),
        ],
        out_shape=[jax.ShapeDtypeStruct(ob.shape, ob.dtype), jax.ShapeDtypeStruct(lse.shape, lse.dtype)],
        input_output_aliases={4: 0, 5: 1},
        compiler_params=_params(("parallel", "parallel", "parallel", "arbitrary")),
        name=f"dilated_{dil}",
    )(pd, pd, pd, tab, ob, lse)


def _flash_c_kernel(lam_ref, q_ref, k_ref, v_ref, tab_ref, g_ref, prev_ref, o_ref,
                    *, scale, out_scale, sub, unroll):
    qb = pl.program_id(2)
    t = q_ref.shape[0]
    c = scale * LOG2E
    reps = sub * t // LANES
    qs = (q_ref[:, :HEAD_DIM], q_ref[:, HEAD_DIM:])

    def body(j, carry):
        off = pl.multiple_of(j * (sub * t), sub * t)
        biases = [tab_ref[jnp.clip(j * sub + u - qb, -2, 2) + 2] for u in range(sub)]
        v = v_ref[pl.ds(off, sub * t), :]
        new = []
        for mi in range(2):
            m, l, acc = carry[mi]
            cols = slice(mi * HEAD_DIM, (mi + 1) * HEAD_DIM)
            s = jnp.concatenate(
                [lax.dot_general(qs[mi], k_ref[pl.ds(off + u * t, t), cols], (((1,), (1,)), ((), ())),
                                 preferred_element_type=F32) * c + biases[u] for u in range(sub)], axis=-1)
            m_new = jnp.maximum(m, jnp.max(s, axis=-1, keepdims=True))
            alpha = jnp.exp2(m - m_new)
            p = jnp.exp2(s - jnp.tile(m_new, (1, reps)))
            ps = p[:, :LANES]
            for i in range(1, reps):
                ps = ps + p[:, i * LANES:(i + 1) * LANES]
            acc = jnp.tile(alpha, (1, 2)) * acc + jnp.dot(p.astype(BF16), v,
                                                        preferred_element_type=F32)
            new.append((m_new, alpha * l + ps, acc))
        return tuple(new)

    one = (jnp.full((t, LANES), -jnp.inf, F32), jnp.zeros((t, LANES), F32),
           jnp.zeros((t, 2 * HEAD_DIM), F32))
    (_, l1, a1), (_, l2, a2) = lax.fori_loop(0, k_ref.shape[0] // (sub * t), body, (one, one),
                                             unroll=unroll)
    o = (a1 / jnp.sum(l1, axis=-1, keepdims=True)
         - lam_ref[0] * (a2 / jnp.sum(l2, axis=-1, keepdims=True)))
    o_ref[...] = (_rms(o, g_ref[...]) * out_scale).astype(o_ref.dtype)


def _flash_c(p, lam, tab, g_c, oc, group, dims, out_scale):
    base, nb, s = group
    heads = dims["c_heads"]
    t = tab.shape[-1]
    nk = s // t
    vw = 2 * HEAD_DIM
    q_cb, k_cb, v_cb = (dims[n] // vw for n in ("qc_off", "kc_off", "vc_off"))
    row = lambda b, i: (base + b * s) // t + i
    kern = functools.partial(_flash_c_kernel, scale=HEAD_DIM ** -0.5, out_scale=out_scale, sub=1,
                             unroll=_largest_unroll(nk))
    return pl.pallas_call(
        kern,
        grid=(nb, heads, nk),
        in_specs=[
            pl.BlockSpec(memory_space=pltpu.SMEM),
            pl.BlockSpec((t, vw), lambda b, h, i: (row(b, i), q_cb + h)),
            pl.BlockSpec((s, vw), lambda b, h, i: (base // s + b, k_cb + h)),
            pl.BlockSpec((s, vw), lambda b, h, i: (base // s + b, v_cb + h)),
            pl.BlockSpec((None, 5, t, t), lambda b, h, i: (h, 0, 0, 0)),
            pl.BlockSpec((None, 1, vw), lambda b, h, i: (h, 0, 0)),
            pl.BlockSpec(memory_space=pl.ANY),
        ],
        out_specs=pl.BlockSpec((t, vw), lambda b, h, i: (row(b, i), h)),
        out_shape=jax.ShapeDtypeStruct(oc.shape, oc.dtype),
        input_output_aliases={6: 0},
        compiler_params=_params(("parallel", "parallel", "arbitrary")),
        name="flash_c",
    )(lam, p, p, p, tab, g_c, oc)


def _mix_kernel(oa_ref, *rest, kv, nbr):
    src_o, src_l = rest[:nbr], rest[nbr:2 * nbr]
    oc_ref, g_ref, out_ref = rest[2 * nbr:2 * nbr + 3]
    tok_o, tok_l = rest[2 * nbr + 3:3 * nbr + 3], rest[3 * nbr + 3:]
    a_w = oa_ref.shape[1]
    b_w = src_o[0].shape[-1]
    tm = oa_ref.shape[0]
    for src, dst in zip(src_o + src_l, tok_o + tok_l):
        dil = src.shape[0]
        for c in range(dst.shape[0]):
            for r in range(dil):
                dst[c, pl.ds(r, tm // dil, stride=dil), :] = src[r, :, c * LANES:(c + 1) * LANES]
    g = g_ref[...]
    out_ref[:, :a_w] = _rms(oa_ref[...], g[:, :a_w]).astype(out_ref.dtype)
    lane = lax.broadcasted_iota(jnp.int32, (tm, LANES), 1)
    heads = []
    for h in range(kv):
        tiles = [r[h] for r in tok_l]
        for gg in range(3):
            ls = [jnp.sum(jnp.where(lane == gg, tl, 0.0), axis=-1, keepdims=True) for tl in tiles]
            mx = jnp.maximum(jnp.maximum(ls[0], ls[1]), ls[2])
            es = [jnp.exp(x - mx) for x in ls]
            den = es[0] + es[1] + es[2]
            heads.append(sum((e / den) * r[h * 3 + gg] for e, r in zip(es, tok_o)))
    ob = jnp.concatenate(heads, axis=-1)
    out_ref[:, a_w:a_w + b_w] = _rms(ob, g[:, a_w:a_w + b_w]).astype(out_ref.dtype)
    out_ref[:, a_w + b_w:] = oc_ref[...]


def _mix(oa, obs, lses, oc, g_out, dims):
    t = oa.shape[0]
    d = g_out.shape[-1]
    tm = 256
    full = lambda a: pl.BlockSpec((tm, a.shape[1]), lambda i: (i, 0))
    res = lambda a: pl.BlockSpec((a.shape[0], tm // a.shape[0], a.shape[2]), lambda i: (0, i, 0))
    return pl.pallas_call(
        functools.partial(_mix_kernel, kv=dims["b_kv"], nbr=len(obs)),
        grid=(t // tm,),
        in_specs=[full(oa)] + [res(a) for a in obs] + [res(a) for a in lses]
                 + [full(oc), pl.BlockSpec((1, d), lambda i: (0, 0))],
        out_specs=pl.BlockSpec((tm, d), lambda i: (i, 0)),
        out_shape=jax.ShapeDtypeStruct((t, d), BF16),
        scratch_shapes=[pltpu.VMEM((a.shape[2] // LANES, tm, LANES), F32) for a in obs + lses],
        compiler_params=_params(("parallel",)),
        name="mix_norm",
    )(oa, *obs, *lses, oc, g_out.reshape(1, d))


def _outproj_kernel(a_ref, w_ref, x_ref, gt_ref, o_ref):
    acc = jnp.dot(a_ref[...], w_ref[...], preferred_element_type=F32)
    o_ref[...] = x_ref[...] + gt_ref[...] * acc


def _outproj(a, w_bf16, x, gt, lay):
    t, d = x.shape
    k = a.shape[1]
    tm, tn = min(512, lay.sp), min(1024, d)
    return pl.pallas_call(
        _outproj_kernel,
        grid=(t // tm, d // tn),
        in_specs=[
            pl.BlockSpec((tm, k), lambda i, j: (i, 0)),
            pl.BlockSpec((k, tn), lambda i, j: (0, j)),
            pl.BlockSpec((tm, tn), lambda i, j: (i, j)),
            pl.BlockSpec((None, 1, tn), lambda i, j: (lay.seq_of(i * tm), 0, j)),
        ],
        out_specs=pl.BlockSpec((tm, tn), lambda i, j: (i, j)),
        out_shape=jax.ShapeDtypeStruct((t, d), F32),
        compiler_params=_params(("parallel", "parallel")),
        name="out_proj",
    )(a, w_bf16, x, gt)


def _router_kernel(x_ref, g_ref, sc_ref, sh_ref, whi_ref, wlo_ref, b_ref, h_ref, gate_ref, idx_ref,
                   slab_ref):
    h = _rms(x_ref[...], g_ref[...]) * (1.0 + sc_ref[...]) + sh_ref[...]
    h_hi = h.astype(BF16)
    h_lo = (h - h_hi.astype(F32)).astype(BF16)
    tm = x_ref.shape[0]
    nc = h_ref.shape[0] // tm
    @pl.when(pl.program_id(0) == 0)
    def _():
        slab_ref[...] = jnp.zeros(slab_ref.shape, slab_ref.dtype)

    for c in range(nc):
        slab_ref[pl.ds(c, tm, stride=ROW_SLAB), :] = h[:, c * LANES:(c + 1) * LANES]
    h_ref[...] = slab_ref[...].reshape(tm, ROW_SLAB, LANES)[:, :nc, :].reshape(tm * nc, LANES)
    logits = (jnp.dot(h_hi, whi_ref[...], preferred_element_type=F32)
              + (jnp.dot(h_hi, wlo_ref[...], preferred_element_type=F32)
                 + jnp.dot(h_lo, whi_ref[...], preferred_element_type=F32))
              + b_ref[...])
    lane = lax.broadcasted_iota(jnp.int32, logits.shape, 1)
    neg = jnp.float32(-jnp.inf)
    is_grp = lane < N_GROUPS
    gl = jnp.where(is_grp, logits, neg)
    gmax = jnp.max(gl, axis=-1, keepdims=True)
    grp = jnp.min(jnp.where(gl == gmax, lane, LANES), axis=-1, keepdims=True)
    p_grp = 1.0 / jnp.sum(jnp.where(is_grp, jnp.exp(gl - gmax), 0.0), axis=-1, keepdims=True)
    lo = N_GROUPS + grp * EXPERTS_PER_GROUP
    el = jnp.where((lane >= lo) & (lane < lo + EXPERTS_PER_GROUP), logits, neg)
    t1 = jnp.max(el, axis=-1, keepdims=True)
    i1 = jnp.min(jnp.where(el == t1, lane, LANES), axis=-1, keepdims=True)
    el2 = jnp.where(lane == i1, neg, el)
    t2 = jnp.max(el2, axis=-1, keepdims=True)
    i2 = jnp.min(jnp.where(el2 == t2, lane, LANES), axis=-1, keepdims=True)
    e = jnp.exp(t2 - t1)
    w1 = p_grp / (1.0 + e)
    w2 = p_grp * e / (1.0 + e)
    gate_ref[...] = jnp.where(lane == 0, w1, jnp.where(lane == 1, w2, 0.0))
    idx_ref[...] = jnp.where(lane == 0, i1 - N_GROUPS, jnp.where(lane == 1, i2 - N_GROUPS, 0))


def _router(x, g, sc, sh, w_hi, w_lo, b_row, lay):
    t, d = x.shape
    tm = 256
    seq = lambda i: (lay.seq_of(i * tm), 0, 0)
    const = lambda i: (0, 0)
    return pl.pallas_call(
        _router_kernel,
        grid=(t // tm,),
        in_specs=[
            pl.BlockSpec((tm, d), lambda i: (i, 0)),
            pl.BlockSpec((1, d), const),
            pl.BlockSpec((None, 1, d), seq),
            pl.BlockSpec((None, 1, d), seq),
            pl.BlockSpec((d, LANES), const),
            pl.BlockSpec((d, LANES), const),
            pl.BlockSpec((1, LANES), const),
        ],
        out_specs=[pl.BlockSpec((tm * (d // LANES), LANES), lambda i: (i, 0)),
                   pl.BlockSpec((tm, LANES), lambda i: (i, 0)),
                   pl.BlockSpec((tm, LANES), lambda i: (i, 0))],
        out_shape=[jax.ShapeDtypeStruct((t * (d // LANES), LANES), F32),
                   jax.ShapeDtypeStruct((t, LANES), F32),
                   jax.ShapeDtypeStruct((t, LANES), jnp.int32)],
        scratch_shapes=[pltpu.VMEM((tm * ROW_SLAB, LANES), F32)],
        compiler_params=_params(("arbitrary",)),
        name="moe_router",
    )(x, g.reshape(1, d), sc, sh, w_hi, w_lo, b_row)


def _gather_kernel(nblk_ref, cur_ref, nxt_ref, h_ref, o_ref, buf_ref, sem):
    i = pl.program_id(0)
    n = pl.num_programs(0)
    rb = o_ref.shape[0]
    nc = h_ref.shape[1]

    def copy(slot, j, tok):
        dst = buf_ref.at[slot, pl.ds(pl.multiple_of(j * ROW_SLAB, 8), nc), :]
        return pltpu.make_async_copy(h_ref.at[tok], dst, sem.at[slot])

    def start_all(idx_ref, slot):
        def body(j2, c):
            for u in range(2):
                j = j2 * 2 + u
                copy(slot, j, idx_ref[0, 0, j]).start(priority=u)
            return c
        lax.fori_loop(0, rb // 2, body, 0, unroll=4)

    @pl.when(i == 0)
    def _():
        start_all(cur_ref, 0)

    @pl.when(jnp.logical_and(i + 1 < n, i + 1 < nblk_ref[0]))
    def _():
        start_all(nxt_ref, (i + 1) % 2)

    @pl.when(i < nblk_ref[0])
    def _():
        slot = i % 2

        def drain(j, c):
            copy(slot, j, 0).wait()
            return c
        lax.fori_loop(0, rb, drain, 0, unroll=8)
        for c in range(nc):
            o_ref[:, c * LANES:(c + 1) * LANES] = (
                buf_ref[slot, pl.ds(c, rb, stride=ROW_SLAB), :].astype(o_ref.dtype))

    @pl.when(i >= nblk_ref[0])
    def _():
        o_ref[...] = jnp.zeros(o_ref.shape, o_ref.dtype)


def _gather_rows(h3, row_tok, n_used_blocks):
    n_rows = row_tok.shape[0]
    _, nc, _ = h3.shape
    rb = MOE_BLOCK
    nb = n_rows // rb
    tok_blocks = row_tok.reshape(nb, 1, rb)
    return pl.pallas_call(
        _gather_kernel,
        grid_spec=pltpu.PrefetchScalarGridSpec(
            num_scalar_prefetch=1,
            grid=(nb,),
            in_specs=[
                pl.BlockSpec((1, 1, rb), lambda i, nu: (i, 0, 0), memory_space=pltpu.SMEM),
                pl.BlockSpec((1, 1, rb), lambda i, nu: (jnp.minimum(i + 1, nb - 1), 0, 0),
                             memory_space=pltpu.SMEM),
                pl.BlockSpec(memory_space=pl.ANY),
            ],
            out_specs=pl.BlockSpec((rb, nc * LANES), lambda i, nu: (i, 0)),
            scratch_shapes=[pltpu.VMEM((2, rb * ROW_SLAB, LANES), h3.dtype),
                            pltpu.SemaphoreType.DMA((2,))],
        ),
        out_shape=jax.ShapeDtypeStruct((n_rows, nc * LANES), BF16),
        compiler_params=_params(("arbitrary",)),
        name="moe_gather",
    )(n_used_blocks, tok_blocks, tok_blocks, h3)


def _expert_ring(bstart_ref, bcount_ref, total_ref, in_copy, out_copy, compute, zero_slot, n_blocks, n_pass):
    n = pl.program_id(0)
    e = pl.program_id(1)
    b0 = bstart_ref[e]
    nb = bcount_ref[e]
    total = total_ref[0]
    last_q = pl.num_programs(0) * total - 1

    @pl.when(jnp.logical_and(nb > 0, jnp.logical_and(n == 0, b0 == 0)))
    def _():
        in_copy(0, 0).start()

    def body(b, carry):
        gb = b0 + b
        q = n * total + gb
        slot = q % 2

        @pl.when(q < last_q)
        def _():
            in_copy(1 - slot, jnp.where(gb + 1 < total, gb + 1, 0)).start()

        in_copy(slot, 0).wait()

        @pl.when(q >= 2)
        def _():
            out_copy(slot, 0, 0).wait()

        compute(slot)
        out_copy(slot, gb, n).start()

        @pl.when(q == last_q)
        def _():
            @pl.when(q >= 1)
            def _():
                out_copy(1 - slot, 0, 0).wait()
            out_copy(slot, 0, 0).wait()

        return carry

    lax.fori_loop(0, nb, body, 0)

    @pl.when(jnp.logical_and(n == pl.num_programs(0) - 1, e == pl.num_programs(1) - 1))
    def _():
        zero_slot(0)

        def fill(gb, carry):
            for n_out in range(n_pass):
                cp = out_copy(0, gb, n_out)
                cp.start()
                cp.wait()
            return carry

        lax.fori_loop(total, n_blocks, fill, 0)


def _moe_up_kernel(bstart_ref, bcount_ref, total_ref, xs_ref, wg_ref, wu_ref, o_ref,
                   xbuf_ref, obuf_ref, wgb_ref, wub_ref, sem_in, sem_out):
    blk = xbuf_ref.shape[1]
    tn = obuf_ref.shape[2]

    @pl.when(bcount_ref[pl.program_id(1)] > 0)
    def _():
        wgb_ref[...] = wg_ref[...].astype(BF16)
        wub_ref[...] = wu_ref[...].astype(BF16)

    def in_copy(slot, gb):
        rows = pl.ds(pl.multiple_of(gb * blk, blk), blk)
        return pltpu.make_async_copy(xs_ref.at[rows], xbuf_ref.at[slot], sem_in.at[slot])

    def out_copy(slot, gb, n):
        rows = pl.ds(pl.multiple_of(gb * blk, blk), blk)
        cols = pl.ds(pl.multiple_of(n * tn, tn), tn)
        return pltpu.make_async_copy(obuf_ref.at[slot], o_ref.at[rows, cols], sem_out.at[slot])

    def compute(slot):
        x = xbuf_ref[slot]
        g = jnp.dot(x, wgb_ref[...], preferred_element_type=F32)
        u = jnp.dot(x, wub_ref[...], preferred_element_type=F32)
        obuf_ref[slot] = (g * jax.nn.sigmoid(g) * u).astype(obuf_ref.dtype)

    def zero_slot(slot):
        obuf_ref[slot] = jnp.zeros(obuf_ref.shape[1:], obuf_ref.dtype)

    _expert_ring(bstart_ref, bcount_ref, total_ref, in_copy, out_copy, compute, zero_slot,
                 o_ref.shape[0] // blk, o_ref.shape[1] // tn)


def _moe_down_kernel(bstart_ref, bcount_ref, total_ref, h_ref, wd_ref, o_ref,
                     hbuf_ref, obuf_ref, wdb_ref, slab_ref, sem_in, sem_out):
    blk = hbuf_ref.shape[1]
    nc = obuf_ref.shape[2]

    @pl.when(bcount_ref[pl.program_id(1)] > 0)
    def _():
        wdb_ref[...] = wd_ref[...].astype(BF16)

    def in_copy(slot, gb):
        rows = pl.ds(pl.multiple_of(gb * blk, blk), blk)
        return pltpu.make_async_copy(h_ref.at[rows], hbuf_ref.at[slot], sem_in.at[slot])

    def out_copy(slot, gb, n):
        rows = pl.ds(pl.multiple_of(gb * blk, blk), blk)
        slabs = pl.ds(pl.multiple_of(n * nc, nc), nc)
        return pltpu.make_async_copy(obuf_ref.at[slot], o_ref.at[rows, slabs], sem_out.at[slot])

    stride = slab_ref.shape[0] // blk

    @pl.when(jnp.logical_and(pl.program_id(0) == 0, pl.program_id(1) == 0))
    def _():
        slab_ref[...] = jnp.zeros(slab_ref.shape, slab_ref.dtype)

    def compute(slot):
        acc = jnp.dot(hbuf_ref[slot], wdb_ref[...], preferred_element_type=F32)
        for c in range(nc):
            slab_ref[pl.ds(c, blk, stride=stride), :] = acc[:, c * LANES:(c + 1) * LANES]
        obuf_ref[slot] = slab_ref[...].reshape(blk, stride, LANES)[:, :nc, :]

    def zero_slot(slot):
        obuf_ref[slot] = jnp.zeros(obuf_ref.shape[1:], obuf_ref.dtype)

    _expert_ring(bstart_ref, bcount_ref, total_ref, in_copy, out_copy, compute, zero_slot,
                 o_ref.shape[0] // blk, o_ref.shape[1] // nc)


def _moe_experts(xs, bstart, bcount, total, w_gate, w_up, w_down, layer):
    n_rows, d = xs.shape
    n_exp, de = w_gate.shape[1], w_gate.shape[-1]
    blk = MOE_BLOCK
    tn1, tn2 = min(512, de), min(2048, d)
    any_spec = pl.BlockSpec(memory_space=pl.ANY)
    wspec = lambda k, tn: pl.BlockSpec((None, None, k, tn), lambda n, e, bs, bc, tot: (layer, e, 0, n))
    dma2 = pltpu.SemaphoreType.DMA((2,))
    hmid = pl.pallas_call(
        _moe_up_kernel,
        grid_spec=pltpu.PrefetchScalarGridSpec(
            num_scalar_prefetch=3,
            grid=(de // tn1, n_exp),
            in_specs=[any_spec, wspec(d, tn1), wspec(d, tn1)],
            out_specs=any_spec,
            scratch_shapes=[pltpu.VMEM((2, blk, d), BF16), pltpu.VMEM((2, blk, tn1), BF16),
                            pltpu.VMEM((d, tn1), BF16), pltpu.VMEM((d, tn1), BF16), dma2, dma2],
        ),
        out_shape=jax.ShapeDtypeStruct((n_rows, de), BF16),
        compiler_params=_params(("arbitrary", "arbitrary")),
        name="moe_up",
    )(bstart, bcount, total, xs, w_gate, w_up)
    nc2 = tn2 // LANES
    return pl.pallas_call(
        _moe_down_kernel,
        grid_spec=pltpu.PrefetchScalarGridSpec(
            num_scalar_prefetch=3,
            grid=(d // tn2, n_exp),
            in_specs=[any_spec, wspec(de, tn2)],
            out_specs=any_spec,
            scratch_shapes=[pltpu.VMEM((2, blk, de), BF16), pltpu.VMEM((2, blk, nc2, LANES), F32),
                            pltpu.VMEM((de, tn2), BF16), pltpu.VMEM((blk * (nc2 + 8), LANES), F32),
                            dma2, dma2],
        ),
        out_shape=jax.ShapeDtypeStruct((n_rows, d // LANES, LANES), F32),
        compiler_params=_params(("arbitrary", "arbitrary")),
        name="moe_down",
    )(bstart, bcount, total, hmid, w_down)


def _combine_kernel(cur_ref, nxt_ref, y_ref, x_ref, gate_ref, gt_ref, gf_ref, *rest, final_norm, split):
    out_refs, (buf_ref, sem) = rest[:-2], rest[-2:]
    i = pl.program_id(0)
    n = pl.num_programs(0)
    tm = x_ref.shape[0]
    rows = TOP_K * tm
    nc = y_ref.shape[1]

    def copy(slot, j, r):
        dst = buf_ref.at[slot, pl.ds(pl.multiple_of(j * ROW_SLAB, 8), nc), :]
        return pltpu.make_async_copy(y_ref.at[r], dst, sem.at[slot])

    def start_all(idx_ref, slot):
        def body(j2, c):
            for u in range(2):
                j = j2 * 2 + u
                copy(slot, j, idx_ref[0, 0, j]).start(priority=u)
            return c
        lax.fori_loop(0, rows // 2, body, 0, unroll=4)

    @pl.when(i == 0)
    def _():
        start_all(cur_ref, 0)

    @pl.when(i + 1 < n)
    def _():
        start_all(nxt_ref, (i + 1) % 2)

    slot = i % 2

    def drain(j, c):
        copy(slot, j, 0).wait()
        return c
    lax.fori_loop(0, rows, drain, 0, unroll=8)

    gate = gate_ref[...]
    lane = lax.broadcasted_iota(jnp.int32, gate.shape, 1)
    w1 = jnp.sum(jnp.where(lane == 0, gate, 0.0), axis=-1, keepdims=True)
    w2 = jnp.sum(jnp.where(lane == 1, gate, 0.0), axis=-1, keepdims=True)
    moe = jnp.concatenate(
        [w1 * buf_ref[slot, pl.ds(c, tm, stride=ROW_SLAB), :]
         + w2 * buf_ref[slot, pl.ds(tm * ROW_SLAB + c, tm, stride=ROW_SLAB), :]
         for c in range(nc)], axis=-1)
    out = x_ref[...] + gt_ref[...] * moe
    if final_norm:
        out = _rms(out, gf_ref[...])
    if split is None:
        out_refs[0][...] = out
    else:
        @pl.when(i < split)
        def _():
            out_refs[0][...] = out

        @pl.when(i >= split)
        def _():
            out_refs[1][...] = out


def _combine(y, dest_blocks, x, gates, gt, g_final, lay, final_norm, split_groups=False):
    t, d = x.shape
    tm = 128
    nb = t // tm
    split = lay.tp // tm if split_groups else None
    kern = functools.partial(_combine_kernel, final_norm=final_norm, split=split)
    if split_groups:
        out_specs = [pl.BlockSpec((tm, d), lambda i: (jnp.minimum(i, split - 1), 0)),
                     pl.BlockSpec((tm, d), lambda i: (jnp.maximum(i - split, 0), 0))]
        out_shape = [jax.ShapeDtypeStruct((lay.tp, d), F32), jax.ShapeDtypeStruct((t - lay.tp, d), F32)]
    else:
        out_specs = pl.BlockSpec((tm, d), lambda i: (i, 0))
        out_shape = jax.ShapeDtypeStruct((t, d), F32)
    return pl.pallas_call(
        kern,
        grid=(nb,),
        in_specs=[
            pl.BlockSpec((1, 1, 2 * tm), lambda i: (i, 0, 0), memory_space=pltpu.SMEM),
            pl.BlockSpec((1, 1, 2 * tm), lambda i: (jnp.minimum(i + 1, nb - 1), 0, 0),
                         memory_space=pltpu.SMEM),
            pl.BlockSpec(memory_space=pl.ANY),
            pl.BlockSpec((tm, d), lambda i: (i, 0)),
            pl.BlockSpec((tm, LANES), lambda i: (i, 0)),
            pl.BlockSpec((None, 1, d), lambda i: (lay.seq_of(i * tm), 0, 0)),
            pl.BlockSpec((1, d), lambda i: (0, 0)),
        ],
        out_specs=out_specs,
        out_shape=out_shape,
        scratch_shapes=[pltpu.VMEM((2, TOP_K * tm * ROW_SLAB, LANES), y.dtype),
                        pltpu.SemaphoreType.DMA((2,))],
        compiler_params=_params(("arbitrary",)),
        name="moe_combine",
    )(dest_blocks, dest_blocks, y, x, gates, gt, g_final.reshape(1, d))


def _t5_bucket(rel):
    half = NUM_BUCKETS // 2
    max_exact = half // 2
    n = jnp.abs(rel)
    large = max_exact + (jnp.log(jnp.maximum(n, max_exact).astype(F32) / max_exact)
                         / math.log(MAX_DISTANCE / max_exact) * (half - max_exact)).astype(jnp.int32)
    large = jnp.minimum(large, half - 1)
    return jnp.where(rel > 0, half, 0) + jnp.where(n < max_exact, n, large)


def _rope_tables(s):
    rows = s // GRID_W
    row = jnp.repeat(jnp.arange(rows, dtype=F32), GRID_W)
    col = jnp.tile(jnp.arange(GRID_W, dtype=F32), rows)
    inv_freq = jnp.exp(-math.log(ROPE_THETA) * jnp.arange(0, AXIS_DIM, 2, dtype=F32) / AXIS_DIM)
    ang_r = row[:, None] * inv_freq[None, :]
    ang_c = col[:, None] * inv_freq[None, :]
    cos = jnp.concatenate([jnp.cos(ang_r)] * 2 + [jnp.cos(ang_c)] * 2, axis=-1)
    sin = jnp.concatenate([-jnp.sin(ang_r), jnp.sin(ang_r), -jnp.sin(ang_c), jnp.sin(ang_c)], axis=-1)
    return cos, sin


def _toeplitz_rel(rows, cols):
    j = jnp.arange(rows + cols, dtype=jnp.int32)
    return jnp.where(j < cols, j, j - (rows + cols))


def _toeplitz(w, rows, cols):
    period = rows + cols
    flat = jnp.tile(w, (1,) * (w.ndim - 1) + (rows,))[..., :rows * (period - 1)]
    return flat.reshape(w.shape[:-1] + (rows, period - 1))[..., :cols]


def _dilated_tables(bias_tab, kv, dil):
    offs = (jnp.arange(2 * B_HALF + 1, dtype=jnp.int32) - B_HALF) * dil
    bias = bias_tab[_t5_bucket(offs)].astype(F32)
    tq = 128
    shift = jnp.array([-B_HALF, 0, -2 * B_HALF], jnp.int32)[:, None]
    rel = _toeplitz_rel(tq, 2 * tq)[None, :] + shift
    vals = jnp.where((jnp.abs(rel) <= B_HALF)[..., None],
                     bias[jnp.clip(rel + B_HALF, 0, 2 * B_HALF)], NEG_INF)
    tab = _toeplitz(jnp.moveaxis(vals, -1, 0), tq, 2 * tq)
    return jnp.swapaxes(tab.reshape(kv, 3, 3, tq, 2 * tq), 1, 2)


def _diff_tables(bias_tab, t):
    rel = _toeplitz_rel(t, t)[None, :] + jnp.array([-t, 0, t], jnp.int32)[:, None]
    near = _toeplitz(jnp.moveaxis(bias_tab[_t5_bucket(rel)].astype(F32), -1, 0), t, t)
    far = bias_tab[_t5_bucket(jnp.array([-2 * t, 2 * t], jnp.int32))].astype(F32).T
    tile = lambda c: jnp.broadcast_to(c[:, None, None, None], (c.shape[0], 1, t, t))
    return jnp.concatenate([tile(far[:, 0]), near, tile(far[:, 1])], axis=1) * LOG2E


def _routing(idx, blk):
    t = idx.shape[0]
    n_assign = t * TOP_K
    flat_e = idx.reshape(-1)
    onehot = (flat_e[:, None] == jnp.arange(N_EXPERTS, dtype=jnp.int32)[None, :]).astype(jnp.int32)
    csum = jnp.cumsum(onehot, axis=0)
    rank = jnp.take_along_axis(csum, flat_e[:, None], axis=1)[:, 0] - 1
    counts = csum[-1]
    padded = (counts + blk - 1) // blk * blk
    pad_end = jnp.cumsum(padded)
    pad_start = pad_end - padded
    dest = (pad_start[flat_e] + rank).astype(jnp.int32)
    n_rows = n_assign + N_EXPERTS * blk
    flat_tok = jnp.arange(n_assign, dtype=jnp.int32) // TOP_K
    row_tok = jnp.zeros((n_rows,), jnp.int32).at[dest].set(flat_tok)
    n_used = (pad_end[-1] // blk).astype(jnp.int32).reshape(1)
    bstart = (pad_start // blk).astype(jnp.int32)
    bcount = (padded // blk).astype(jnp.int32)
    return dest.reshape(t, TOP_K), row_tok, bstart, bcount, n_used


def _dims(d):
    slots = d // HEAD_DIM
    a_heads = 3 * slots // 8
    a_kv = a_heads // 3
    b_heads = 3 * slots // 8
    b_kv = b_heads // 3
    c_heads = slots // 8
    c_w = c_heads * 2 * HEAD_DIM
    offs, acc = [], 0
    for n in (b_heads, b_kv, b_kv, a_heads, a_kv, a_kv):
        offs.append(acc)
        acc += n * HEAD_DIM
    for n in (c_w, c_w, c_w):
        offs.append(acc)
        acc += n
    names = ("qb_off", "kb_off", "vb_off", "qa_off", "ka_off", "va_off", "qc_off", "kc_off", "vc_off")
    out = dict(zip(names, offs))
    out.update(a_heads=a_heads, a_kv=a_kv, a_grp=3, b_heads=b_heads, b_kv=b_kv, c_heads=c_heads,
               a_w=a_heads * HEAD_DIM, b_w=b_heads * HEAD_DIM, c_w=c_w, in_w=acc)
    return out


def kernel(x_prompt, x_sample, c_prompt, c_sample, rel_bias, w_ada, b_ada, g_norm1, w_in, g_qk, lam_c,
           g_out, w_out, g_norm2, w_group, b_group, w_router, b_router, w_gate, w_up, w_down, g_final):
    bp, sp, d = x_prompt.shape
    bs, ss, _ = x_sample.shape
    depth = w_in.shape[0]
    lay = _Layout(bp, sp, bs, ss)
    dims = _dims(d)
    t = lay.t
    x = jnp.concatenate([x_prompt.reshape(bp * sp, d), x_sample.reshape(bs * ss, d)], axis=0)

    c_all = jnp.concatenate([c_prompt, c_sample], axis=0)
    pad = (-c_all.shape[0]) % 8
    c_pad = jnp.pad(c_all, ((0, pad), (0, 0)))
    mod = _ada_mod(c_pad, w_ada, b_ada)[:, :lay.nseq].reshape(depth, lay.nseq, 6, 1, d)

    rope_cos, rope_sin = _rope_tables(max(sp, ss))
    qk_w = dims["va_off"] - dims["qa_off"]
    b_tabs = [_dilated_tables(rel_bias[:, :dims["b_heads"]], dims["b_kv"], dil) for _, dil in B_BRANCHES]
    c_t = min(512, sp, ss)
    c_tab = _diff_tables(rel_bias[:, dims["b_heads"]:], c_t)

    for l in range(depth):
        lambda_init = 0.8 - 0.6 * math.exp(-0.3 * l)
        sh1, sc1, gt1, sh2, sc2, gt2 = (mod[l, :, i] for i in range(6))

        a_end = dims["a_w"] + 2 * dims["a_kv"] * HEAD_DIM
        b_end = a_end + dims["qa_off"]
        w_l = jnp.concatenate([w_in[l, :, a_end:b_end], w_in[l, :, :a_end], w_in[l, :, b_end:]],
                              axis=1).astype(BF16)
        p, *p_res = _inproj(_normmod(x, g_norm1[l], sc1, sh1, lay), w_l, lay, dims)
        p_by_dil = [p.reshape(1, t, p.shape[1])] + p_res
        g_row = jnp.concatenate([jnp.tile(g_qk[l, 0], dims["a_heads"]),
                                 jnp.tile(g_qk[l, 1], dims["a_kv"])]).reshape(1, qk_w)
        qk = _aprep(p, g_row, rope_cos, rope_sin, lay, dims["qa_off"], qk_w)

        lam = lam_c[l].astype(F32)
        lam_val = (jnp.exp(jnp.sum(lam[0] * lam[1])) - jnp.exp(jnp.sum(lam[2] * lam[3])) + lambda_init)
        lam_val = lam_val.reshape(1).astype(F32)
        g_c = g_out[l, dims["a_w"] + dims["b_w"]:].reshape(dims["c_heads"], 1, 2 * HEAD_DIM)

        if l == 0:
            oa = jnp.zeros((t, dims["a_w"]), F32)
            oc = jnp.zeros((t, dims["c_w"]), BF16)
            obs = [jnp.zeros((dil, t // dil, dims["b_w"]), F32) for _, dil in B_BRANCHES]
            lses = [jnp.zeros((dil, t // dil, dims["b_kv"] * LANES), F32) for _, dil in B_BRANCHES]
        for group in lay.groups:
            oa = _flash_a(qk, p, oa, group, dims)
            oc = _flash_c(p, lam_val, c_tab, g_c, oc, group, dims, 1.0 - lambda_init)
            for n, (_, dil) in enumerate(B_BRANCHES):
                obs[n], lses[n] = _dilated_branch(p_by_dil[n], b_tabs[n], obs[n], lses[n], group, dims, dil)
        mixed = _mix(oa, obs, lses, oc, g_out[l], dims)
        x = _outproj(mixed, w_out[l].astype(BF16), x, gt1, lay)

        w_r = jnp.concatenate([w_group[l], w_router[l],
                               jnp.zeros((d, LANES - N_GROUPS - N_EXPERTS), F32)], axis=1)
        w_hi = w_r.astype(BF16)
        w_lo = (w_r - w_hi.astype(F32)).astype(BF16)
        b_row = jnp.concatenate([b_group[l], b_router[l],
                                 jnp.zeros((LANES - N_GROUPS - N_EXPERTS,), F32)]).reshape(1, LANES)
        h2, gates, idx = _router(x, g_norm2[l], sc2, sh2, w_hi, w_lo, b_row.astype(F32), lay)
        dest, row_tok, bstart, bcount, n_used = _routing(idx[:, :TOP_K], MOE_BLOCK)
        xs = _gather_rows(h2.reshape(t, d // LANES, LANES), row_tok, n_used)
        y = _moe_experts(xs, bstart, bcount, n_used, w_gate, w_up, w_down, l)
        tmc = 128
        dest_blocks = dest.reshape(t // tmc, tmc, TOP_K).transpose(0, 2, 1).reshape(t // tmc, 1, TOP_K * tmc)
        last = l == depth - 1
        x = _combine(y, dest_blocks, x, gates, gt2, g_final, lay, final_norm=last, split_groups=last)

    return (x[0].reshape(bp, sp, d), x[1].reshape(bs, ss, d))
```

```python
import functools
import math

import jax
import jax.numpy as jnp
from jax import lax
from jax.experimental import pallas as pl
from jax.experimental.pallas import tpu as pltpu

F32 = jnp.float32
BF16 = jnp.bfloat16

HEAD_DIM = 128
LANES = 128
GRID_W = 64
AXIS_DIM = HEAD_DIM // 2
ROPE_THETA = 10000.0
B_BRANCHES = ((128, 1), (512, 4), (2048, 16))
B_HALF = 64
NUM_BUCKETS = 32
MAX_DISTANCE = 128
N_GROUPS = 8
EXPERTS_PER_GROUP = 8
N_EXPERTS = N_GROUPS * EXPERTS_PER_GROUP
TOP_K = 2
NORM_EPS = 1e-6
NEG_INF = -1e30
LOG2E = math.log2(math.e)
VMEM_LIMIT_BYTES = 56 * 1024 * 1024
MOE_BLOCK = 256
ROW_SLAB = 40


def _params(sem):
    return pltpu.CompilerParams(dimension_semantics=sem, vmem_limit_bytes=VMEM_LIMIT_BYTES)


class _Layout:
    def __init__(self, bp, sp, bs, ss):
        self.bp, self.sp, self.bs, self.ss = bp, sp, bs, ss
        self.tp = bp * sp
        self.t = bp * sp + bs * ss
        self.nseq = bp + bs
        self.groups = ((0, bp, sp), (self.tp, bs, ss))

    def seq_of(self, row0):
        return jnp.where(row0 < self.tp, row0 // self.sp, self.bp + (row0 - self.tp) // self.ss)

    def pos_of(self, row0):
        return jnp.where(row0 < self.tp, row0 % self.sp, (row0 - self.tp) % self.ss)


def _largest_unroll(trips, cap=8):
    u = cap
    while trips % u:
        u //= 2
    return u


def _rms(x, g):
    var = jnp.mean(x * x, axis=-1, keepdims=True)
    return x * lax.rsqrt(var + NORM_EPS) * g


def _ada_kernel(c_ref, w_ref, b_ref, o_ref):
    c = c_ref[...]
    cs = c * jax.nn.sigmoid(c)
    o_ref[...] = jnp.dot(cs, w_ref[...], preferred_element_type=F32) + b_ref[...]


def _ada_mod(c_all, w_ada, b_ada):
    depth, d, n = w_ada.shape
    rows = c_all.shape[0]
    tn = min(512, n)
    return pl.pallas_call(
        _ada_kernel,
        grid=(depth, n // tn),
        in_specs=[
            pl.BlockSpec((rows, d), lambda l, j: (0, 0)),
            pl.BlockSpec((None, d, tn), lambda l, j: (l, 0, j)),
            pl.BlockSpec((None, 1, tn), lambda l, j: (l, 0, j)),
        ],
        out_specs=pl.BlockSpec((None, rows, tn), lambda l, j: (l, 0, j)),
        out_shape=jax.ShapeDtypeStruct((depth, rows, n), F32),
        compiler_params=_params(("parallel", "parallel")),
        name="ada_mod",
    )(c_all, w_ada, b_ada.reshape(depth, 1, n))


def _normmod_kernel(x_ref, g_ref, sc_ref, sh_ref, o_ref):
    h = _rms(x_ref[...], g_ref[...]) * (1.0 + sc_ref[...]) + sh_ref[...]
    o_ref[...] = h.astype(o_ref.dtype)


def _normmod(x, g, sc, sh, lay):
    t, d = x.shape
    tm = min(512, lay.sp)
    seq = lambda i: (lay.seq_of(i * tm), 0, 0)
    return pl.pallas_call(
        _normmod_kernel,
        grid=(t // tm,),
        in_specs=[
            pl.BlockSpec((tm, d), lambda i: (i, 0)),
            pl.BlockSpec((1, d), lambda i: (0, 0)),
            pl.BlockSpec((None, 1, d), seq),
            pl.BlockSpec((None, 1, d), seq),
        ],
        out_specs=pl.BlockSpec((tm, d), lambda i: (i, 0)),
        out_shape=jax.ShapeDtypeStruct((t, d), BF16),
        compiler_params=_params(("parallel",)),
        name="norm_mod",
    )(x, g.reshape(1, d), sc, sh)


def _inproj_kernel(h_ref, w_ref, o_ref, *rest, dils, nb_tiles):
    res_refs, (acc_ref,) = rest[:len(dils)], rest[len(dils):]
    j = pl.program_id(1)
    tm = h_ref.shape[0]

    acc = jnp.dot(h_ref[...], w_ref[...], preferred_element_type=F32)
    o_ref[...] = acc.astype(o_ref.dtype)

    @pl.when(j < nb_tiles)
    def _():
        for c in range(acc_ref.shape[0]):
            cols = slice(c * LANES, (c + 1) * LANES)
            acc_ref[c] = acc[:, cols]
            for dil, r_ref in zip(dils, res_refs):
                for r in range(dil):
                    r_ref[r, :, cols] = acc_ref[c, pl.ds(r, tm // dil, stride=dil), :].astype(r_ref.dtype)


def _inproj(h, w_bf16, lay, dims):
    t, d = h.shape
    n = w_bf16.shape[1]
    tm, tn = min(1024, lay.sp), d // 8
    bw = dims["qa_off"]
    nb_tiles = bw // tn
    assert nb_tiles * tn == bw
    dils = tuple(dil for _, dil in B_BRANCHES if dil > 1)
    kern = functools.partial(_inproj_kernel, dils=dils, nb_tiles=nb_tiles)
    return pl.pallas_call(
        kern,
        grid=(t // tm, n // tn),
        in_specs=[
            pl.BlockSpec((tm, d), lambda i, j: (i, 0)),
            pl.BlockSpec((d, tn), lambda i, j: (0, j)),
        ],
        out_specs=[pl.BlockSpec((tm, tn), lambda i, j: (i, j))]
                  + [pl.BlockSpec((dil, tm // dil, tn), lambda i, j: (0, i, jnp.minimum(j, nb_tiles - 1)))
                     for dil in dils],
        out_shape=[jax.ShapeDtypeStruct((t, n), BF16)]
                  + [jax.ShapeDtypeStruct((dil, t // dil, bw), BF16) for dil in dils],
        scratch_shapes=[pltpu.VMEM((tn // LANES, tm, LANES), F32)],
        compiler_params=_params(("parallel", "arbitrary")),
        name="in_proj",
    )(h, w_bf16)


def _aprep_kernel(p_ref, g_ref, cos_ref, sin_ref, o_ref):
    cos = cos_ref[...]
    sin = sin_ref[...]
    lane = lax.broadcasted_iota(jnp.int32, cos.shape, 1)
    first_half = (lane % AXIS_DIM) < (AXIS_DIM // 2)
    for h in range(p_ref.shape[1] // HEAD_DIM):
        sl = slice(h * HEAD_DIM, (h + 1) * HEAD_DIM)
        y = _rms(p_ref[:, sl].astype(F32), g_ref[:, sl])
        partner = jnp.where(first_half,
                            pltpu.roll(y, HEAD_DIM - AXIS_DIM // 2, 1),
                            pltpu.roll(y, AXIS_DIM // 2, 1))
        o_ref[:, sl] = (y * cos + partner * sin).astype(o_ref.dtype)


def _aprep(p, g_row, rope_cos, rope_sin, lay, col0, width):
    t = p.shape[0]
    tm, cw = min(512, lay.sp), width // 4
    assert col0 % cw == 0
    pos = lambda i, j: (lay.pos_of(i * tm) // tm, 0)
    return pl.pallas_call(
        _aprep_kernel,
        grid=(t // tm, width // cw),
        in_specs=[
            pl.BlockSpec((tm, cw), lambda i, j: (i, col0 // cw + j)),
            pl.BlockSpec((1, cw), lambda i, j: (0, j)),
            pl.BlockSpec((tm, HEAD_DIM), pos),
            pl.BlockSpec((tm, HEAD_DIM), pos),
        ],
        out_specs=pl.BlockSpec((tm, cw), lambda i, j: (i, j)),
        out_shape=jax.ShapeDtypeStruct((t, width), BF16),
        compiler_params=_params(("parallel", "parallel")),
        name="a_prep",
    )(p, g_row, rope_cos, rope_sin)


def _flash_a_kernel(q_ref, k_ref, v_ref, prev_ref, o_ref, qs_ref, *, scale, grp, tk, unroll):
    tq = q_ref.shape[0]
    rows = grp * tq
    for g in range(grp):
        qs_ref[g * tq:(g + 1) * tq, :] = q_ref[:, g * HEAD_DIM:(g + 1) * HEAD_DIM]
    q = qs_ref[...]
    c = scale * LOG2E
    reps = tk // LANES

    def body(j, carry):
        m, l, acc = carry
        off = pl.multiple_of(j * tk, tk)
        s = lax.dot_general(q, k_ref[pl.ds(off, tk), :], (((1,), (1,)), ((), ())),
                            preferred_element_type=F32)
        m_new = jnp.maximum(m, jnp.max(s, axis=-1, keepdims=True))
        alpha = jnp.exp2((m - m_new) * c)
        p = jnp.exp2((s - jnp.tile(m_new, (1, reps))) * c)
        ps = p[:, :LANES]
        for i in range(1, reps):
            ps = ps + p[:, i * LANES:(i + 1) * LANES]
        acc = alpha * acc + jnp.dot(p.astype(BF16), v_ref[pl.ds(off, tk), :],
                                    preferred_element_type=F32)
        return m_new, alpha * l + ps, acc

    init = (jnp.full((rows, LANES), -jnp.inf, F32), jnp.zeros((rows, LANES), F32),
            jnp.zeros((rows, HEAD_DIM), F32))
    _, l, acc = lax.fori_loop(0, k_ref.shape[0] // tk, body, init, unroll=unroll)
    out = acc / jnp.sum(l, axis=-1, keepdims=True)
    for g in range(grp):
        o_ref[:, g * HEAD_DIM:(g + 1) * HEAD_DIM] = out[g * tq:(g + 1) * tq].astype(o_ref.dtype)


def _flash_a(qk, p, oa, group, dims):
    base, nb, s = group
    grp = dims["a_grp"]
    kvh = dims["a_kv"]
    tq, tk = min(256, s), min(1024, s)
    nk = s // tk
    qw = grp * HEAD_DIM
    k_col0 = dims["a_heads"]
    v_col0 = dims["va_off"] // HEAD_DIM
    kern = functools.partial(_flash_a_kernel, scale=HEAD_DIM ** -0.5, grp=grp, tk=tk,
                             unroll=_largest_unroll(nk))
    out = pl.pallas_call(
        kern,
        grid=(nb, kvh, s // tq),
        in_specs=[
            pl.BlockSpec((tq, qw), lambda b, h, i: ((base + b * s) // tq + i, h)),
            pl.BlockSpec((s, HEAD_DIM), lambda b, h, i: (base // s + b, k_col0 + h)),
            pl.BlockSpec((s, HEAD_DIM), lambda b, h, i: (base // s + b, v_col0 + h)),
            pl.BlockSpec(memory_space=pl.ANY),
        ],
        out_specs=pl.BlockSpec((tq, qw), lambda b, h, i: ((base + b * s) // tq + i, h)),
        out_shape=jax.ShapeDtypeStruct(oa.shape, oa.dtype),
        scratch_shapes=[pltpu.VMEM((grp * tq, HEAD_DIM), BF16)],
        input_output_aliases={3: 0},
        compiler_params=_params(("parallel", "parallel", "arbitrary")),
        name="flash_a",
    )
    return out(qk, qk, p, oa)


def _dil_kernel(q_ref, k_ref, v_ref, tab_ref, prev_o_ref, prev_l_ref, o_ref, lse_ref, *, scale, tq):
    nsub = q_ref.shape[0] // tq
    nib = pl.num_programs(3) * nsub
    win = tab_ref.shape[-1]
    n_d = k_ref.shape[0]
    lane = lax.broadcasted_iota(jnp.int32, (tq, LANES), 1)
    for u in range(nsub):
        ib = pl.program_id(3) * nsub + u
        rows = slice(u * tq, (u + 1) * tq)
        ws = pl.multiple_of(jnp.clip(ib * tq - B_HALF, 0, n_d - win), B_HALF)
        variant = jnp.where(ib == 0, 1, jnp.where(ib == nib - 1, 2, 0))
        kw = k_ref[pl.ds(ws, win), :]
        vw = v_ref[pl.ds(ws, win), :]
        lse_tile = jnp.zeros((tq, LANES), F32)
        for g in range(3):
            cols = slice(g * HEAD_DIM, (g + 1) * HEAD_DIM)
            s = lax.dot_general(q_ref[rows, cols], kw, (((1,), (1,)), ((), ())),
                                preferred_element_type=F32) * scale + tab_ref[variant, g]
            m = jnp.max(s, axis=-1, keepdims=True)
            p = jnp.exp(s - m)
            l = jnp.sum(p, axis=-1, keepdims=True)
            o_ref[rows, cols] = jnp.dot(p.astype(BF16), vw, preferred_element_type=F32) / l
            lse_tile = jnp.where(lane == g, m + jnp.log(l), lse_tile)
        lse_ref[rows, :] = lse_tile


def _dilated_branch(pd, tab, ob, lse, group, dims, dil):
    base, nb, s = group
    kv = dims["b_kv"]
    n_d = s // dil
    tq = 128
    tb = tq * min(4, n_d // tq)
    assert n_d >= 2 * tq and base % s == 0 and tab.shape[-1] == 2 * tq and pd.shape[0] == dil
    q_cb = dims["qb_off"] // (3 * HEAD_DIM)
    k_cb, v_cb = (dims[n] // HEAD_DIM for n in ("kb_off", "vb_off"))
    row = lambda b, i: (base // dil + b * n_d) // tb + i
    kern = functools.partial(_dil_kernel, scale=HEAD_DIM ** -0.5, tq=tq)
    return pl.pallas_call(
        kern,
        grid=(nb, kv, dil, n_d // tb),
        in_specs=[
            pl.BlockSpec((None, tb, 3 * HEAD_DIM), lambda b, h, r, i: (r, row(b, i), q_cb + h)),
            pl.BlockSpec((None, n_d, HEAD_DIM), lambda b, h, r, i: (r, base // s + b, k_cb + h)),
            pl.BlockSpec((None, n_d, HEAD_DIM), lambda b, h, r, i: (r, base // s + b, v_cb + h)),
            pl.BlockSpec((None, 3, 3, tq, 2 * tq), lambda b, h, r, i: (h, 0, 0, 0, 0)),
            pl.BlockSpec(memory_space=pl.ANY),
            pl.BlockSpec(memory_space=pl.ANY),
        ],
        out_specs=[
            pl.BlockSpec((None, tb, 3 * HEAD_DIM), lambda b, h, r, i: (r, row(b, i), h)),
            pl.BlockSpec((None, tb, LANES), lambda b, h, r, i: (r, row(b, i), h)),
        ],
        out_shape=[jax.ShapeDtypeStruct(ob.shape, ob.dtype), jax.ShapeDtypeStruct(lse.shape, lse.dtype)],
        input_output_aliases={4: 0, 5: 1},
        compiler_params=_params(("parallel", "parallel", "parallel", "arbitrary")),
        name=f"dilated_{dil}",
    )(pd, pd, pd, tab, ob, lse)


def _flash_c_kernel(lam_ref, q_ref, k_ref, v_ref, tab_ref, g_ref, prev_ref, o_ref,
                    *, scale, out_scale, sub, unroll):
    qb = pl.program_id(2)
    t = q_ref.shape[0]
    c = scale * LOG2E
    reps = sub * t // LANES
    qs = (q_ref[:, :HEAD_DIM], q_ref[:, HEAD_DIM:])

    def body(j, carry):
        off = pl.multiple_of(j * (sub * t), sub * t)
        biases = [tab_ref[jnp.clip(j * sub + u - qb, -2, 2) + 2] for u in range(sub)]
        v = v_ref[pl.ds(off, sub * t), :]
        new = []
        for mi in range(2):
            m, l, acc = carry[mi]
            cols = slice(mi * HEAD_DIM, (mi + 1) * HEAD_DIM)
            s = jnp.concatenate(
                [lax.dot_general(qs[mi], k_ref[pl.ds(off + u * t, t), cols], (((1,), (1,)), ((), ())),
                                 preferred_element_type=F32) * c + biases[u] for u in range(sub)], axis=-1)
            m_new = jnp.maximum(m, jnp.max(s, axis=-1, keepdims=True))
            alpha = jnp.exp2(m - m_new)
            p = jnp.exp2(s - jnp.tile(m_new, (1, reps)))
            ps = p[:, :LANES]
            for i in range(1, reps):
                ps = ps + p[:, i * LANES:(i + 1) * LANES]
            acc = jnp.tile(alpha, (1, 2)) * acc + jnp.dot(p.astype(BF16), v,
                                                        preferred_element_type=F32)
            new.append((m_new, alpha * l + ps, acc))
        return tuple(new)

    one = (jnp.full((t, LANES), -jnp.inf, F32), jnp.zeros((t, LANES), F32),
           jnp.zeros((t, 2 * HEAD_DIM), F32))
    (_, l1, a1), (_, l2, a2) = lax.fori_loop(0, k_ref.shape[0] // (sub * t), body, (one, one),
                                             unroll=unroll)
    o = (a1 / jnp.sum(l1, axis=-1, keepdims=True)
         - lam_ref[0] * (a2 / jnp.sum(l2, axis=-1, keepdims=True)))
    o_ref[...] = (_rms(o, g_ref[...]) * out_scale).astype(o_ref.dtype)


def _flash_c(p, lam, tab, g_c, oc, group, dims, out_scale):
    base, nb, s = group
    heads = dims["c_heads"]
    t = tab.shape[-1]
    nk = s // t
    vw = 2 * HEAD_DIM
    q_cb, k_cb, v_cb = (dims[n] // vw for n in ("qc_off", "kc_off", "vc_off"))
    row = lambda b, i: (base + b * s) // t + i
    kern = functools.partial(_flash_c_kernel, scale=HEAD_DIM ** -0.5, out_scale=out_scale, sub=1,
                             unroll=_largest_unroll(nk))
    return pl.pallas_call(
        kern,
        grid=(nb, heads, nk),
        in_specs=[
            pl.BlockSpec(memory_space=pltpu.SMEM),
            pl.BlockSpec((t, vw), lambda b, h, i: (row(b, i), q_cb + h)),
            pl.BlockSpec((s, vw), lambda b, h, i: (base // s + b, k_cb + h)),
            pl.BlockSpec((s, vw), lambda b, h, i: (base // s + b, v_cb + h)),
            pl.BlockSpec((None, 5, t, t), lambda b, h, i: (h, 0, 0, 0)),
            pl.BlockSpec((None, 1, vw), lambda b, h, i: (h, 0, 0)),
            pl.BlockSpec(memory_space=pl.ANY),
        ],
        out_specs=pl.BlockSpec((t, vw), lambda b, h, i: (row(b, i), h)),
        out_shape=jax.ShapeDtypeStruct(oc.shape, oc.dtype),
        input_output_aliases={6: 0},
        compiler_params=_params(("parallel", "parallel", "arbitrary")),
        name="flash_c",
    )(lam, p, p, p, tab, g_c, oc)


def _mix_kernel(oa_ref, *rest, kv, nbr):
    src_o, src_l = rest[:nbr], rest[nbr:2 * nbr]
    oc_ref, g_ref, out_ref = rest[2 * nbr:2 * nbr + 3]
    tok_o, tok_l = rest[2 * nbr + 3:3 * nbr + 3], rest[3 * nbr + 3:]
    a_w = oa_ref.shape[1]
    b_w = src_o[0].shape[-1]
    tm = oa_ref.shape[0]
    for src, dst in zip(src_o + src_l, tok_o + tok_l):
        dil = src.shape[0]
        for c in range(dst.shape[0]):
            for r in range(dil):
                dst[c, pl.ds(r, tm // dil, stride=dil), :] = src[r, :, c * LANES:(c + 1) * LANES]
    g = g_ref[...]
    out_ref[:, :a_w] = _rms(oa_ref[...], g[:, :a_w]).astype(out_ref.dtype)
    lane = lax.broadcasted_iota(jnp.int32, (tm, LANES), 1)
    heads = []
    for h in range(kv):
        tiles = [r[h] for r in tok_l]
        for gg in range(3):
            ls = [jnp.sum(jnp.where(lane == gg, tl, 0.0), axis=-1, keepdims=True) for tl in tiles]
            mx = jnp.maximum(jnp.maximum(ls[0], ls[1]), ls[2])
            es = [jnp.exp(x - mx) for x in ls]
            den = es[0] + es[1] + es[2]
            heads.append(sum((e / den) * r[h * 3 + gg] for e, r in zip(es, tok_o)))
    ob = jnp.concatenate(heads, axis=-1)
    out_ref[:, a_w:a_w + b_w] = _rms(ob, g[:, a_w:a_w + b_w]).astype(out_ref.dtype)
    out_ref[:, a_w + b_w:] = oc_ref[...]


def _mix(oa, obs, lses, oc, g_out, dims):
    t = oa.shape[0]
    d = g_out.shape[-1]
    tm = 256
    full = lambda a: pl.BlockSpec((tm, a.shape[1]), lambda i: (i, 0))
    res = lambda a: pl.BlockSpec((a.shape[0], tm // a.shape[0], a.shape[2]), lambda i: (0, i, 0))
    return pl.pallas_call(
        functools.partial(_mix_kernel, kv=dims["b_kv"], nbr=len(obs)),
        grid=(t // tm,),
        in_specs=[full(oa)] + [res(a) for a in obs] + [res(a) for a in lses]
                 + [full(oc), pl.BlockSpec((1, d), lambda i: (0, 0))],
        out_specs=pl.BlockSpec((tm, d), lambda i: (i, 0)),
        out_shape=jax.ShapeDtypeStruct((t, d), BF16),
        scratch_shapes=[pltpu.VMEM((a.shape[2] // LANES, tm, LANES), F32) for a in obs + lses],
        compiler_params=_params(("parallel",)),
        name="mix_norm",
    )(oa, *obs, *lses, oc, g_out.reshape(1, d))


def _outproj_kernel(a_ref, w_ref, x_ref, gt_ref, o_ref):
    acc = jnp.dot(a_ref[...], w_ref[...], preferred_element_type=F32)
    o_ref[...] = x_ref[...] + gt_ref[...] * acc


def _outproj(a, w_bf16, x, gt, lay):
    t, d = x.shape
    k = a.shape[1]
    tm, tn = min(512, lay.sp), min(1024, d)
    return pl.pallas_call(
        _outproj_kernel,
        grid=(t // tm, d // tn),
        in_specs=[
            pl.BlockSpec((tm, k), lambda i, j: (i, 0)),
            pl.BlockSpec((k, tn), lambda i, j: (0, j)),
            pl.BlockSpec((tm, tn), lambda i, j: (i, j)),
            pl.BlockSpec((None, 1, tn), lambda i, j: (lay.seq_of(i * tm), 0, j)),
        ],
        out_specs=pl.BlockSpec((tm, tn), lambda i, j: (i, j)),
        out_shape=jax.ShapeDtypeStruct((t, d), F32),
        compiler_params=_params(("parallel", "parallel")),
        name="out_proj",
    )(a, w_bf16, x, gt)


def _router_kernel(x_ref, g_ref, sc_ref, sh_ref, whi_ref, wlo_ref, b_ref, h_ref, gate_ref, idx_ref,
                   slab_ref):
    tm, d = x_ref.shape
    nc = h_ref.shape[0] // tm
    ck = min(512, d)

    @pl.when(pl.program_id(0) == 0)
    def _():
        slab_ref[...] = jnp.zeros(slab_ref.shape, slab_ref.dtype)

    ssq = jnp.zeros((tm, 1), F32)
    for c0 in range(0, d, ck):
        xc = x_ref[:, c0:c0 + ck]
        ssq = ssq + jnp.sum(xc * xc, axis=-1, keepdims=True)
    inv = lax.rsqrt(ssq / d + NORM_EPS)
    logits = b_ref[...]
    for c0 in range(0, d, ck):
        cs = slice(c0, c0 + ck)
        h = x_ref[:, cs] * inv * g_ref[:, cs] * (1.0 + sc_ref[:, cs]) + sh_ref[:, cs]
        h_hi = h.astype(BF16)
        h_lo = (h - h_hi.astype(F32)).astype(BF16)
        logits = logits + (jnp.dot(h_hi, whi_ref[cs, :], preferred_element_type=F32)
                           + (jnp.dot(h_hi, wlo_ref[cs, :], preferred_element_type=F32)
                              + jnp.dot(h_lo, whi_ref[cs, :], preferred_element_type=F32)))
        for c in range(ck // LANES):
            slab_ref[pl.ds(c0 // LANES + c, tm, stride=ROW_SLAB), :] = h[:, c * LANES:(c + 1) * LANES]
    h_ref[...] = slab_ref[...].reshape(tm, ROW_SLAB, LANES)[:, :nc, :].reshape(tm * nc, LANES)
    lane = lax.broadcasted_iota(jnp.int32, logits.shape, 1)
    neg = jnp.float32(-jnp.inf)
    is_grp = lane < N_GROUPS
    gl = jnp.where(is_grp, logits, neg)
    gmax = jnp.max(gl, axis=-1, keepdims=True)
    grp = jnp.min(jnp.where(gl == gmax, lane, LANES), axis=-1, keepdims=True)
    p_grp = 1.0 / jnp.sum(jnp.where(is_grp, jnp.exp(gl - gmax), 0.0), axis=-1, keepdims=True)
    lo = N_GROUPS + grp * EXPERTS_PER_GROUP
    el = jnp.where((lane >= lo) & (lane < lo + EXPERTS_PER_GROUP), logits, neg)
    t1 = jnp.max(el, axis=-1, keepdims=True)
    i1 = jnp.min(jnp.where(el == t1, lane, LANES), axis=-1, keepdims=True)
    el2 = jnp.where(lane == i1, neg, el)
    t2 = jnp.max(el2, axis=-1, keepdims=True)
    i2 = jnp.min(jnp.where(el2 == t2, lane, LANES), axis=-1, keepdims=True)
    e = jnp.exp(t2 - t1)
    w1 = p_grp / (1.0 + e)
    w2 = p_grp * e / (1.0 + e)
    gate_ref[...] = jnp.where(lane == 0, w1, jnp.where(lane == 1, w2, 0.0))
    idx_ref[...] = jnp.where(lane == 0, i1 - N_GROUPS, jnp.where(lane == 1, i2 - N_GROUPS, 0))


def _router(x, g, sc, sh, w_hi, w_lo, b_row, lay):
    t, d = x.shape
    tm = 256
    seq = lambda i: (lay.seq_of(i * tm), 0, 0)
    const = lambda i: (0, 0)
    return pl.pallas_call(
        _router_kernel,
        grid=(t // tm,),
        in_specs=[
            pl.BlockSpec((tm, d), lambda i: (i, 0)),
            pl.BlockSpec((1, d), const),
            pl.BlockSpec((None, 1, d), seq),
            pl.BlockSpec((None, 1, d), seq),
            pl.BlockSpec((d, LANES), const),
            pl.BlockSpec((d, LANES), const),
            pl.BlockSpec((1, LANES), const),
        ],
        out_specs=[pl.BlockSpec((tm * (d // LANES), LANES), lambda i: (i, 0)),
                   pl.BlockSpec((tm, LANES), lambda i: (i, 0)),
                   pl.BlockSpec((tm, LANES), lambda i: (i, 0))],
        out_shape=[jax.ShapeDtypeStruct((t * (d // LANES), LANES), F32),
                   jax.ShapeDtypeStruct((t, LANES), F32),
                   jax.ShapeDtypeStruct((t, LANES), jnp.int32)],
        scratch_shapes=[pltpu.VMEM((tm * ROW_SLAB, LANES), F32)],
        compiler_params=_params(("arbitrary",)),
        name="moe_router",
    )(x, g.reshape(1, d), sc, sh, w_hi, w_lo, b_row)


def _gather_kernel(nblk_ref, cur_ref, nxt_ref, h_ref, o_ref, buf_ref, sem):
    i = pl.program_id(0)
    n = pl.num_programs(0)
    rb = o_ref.shape[0]
    nc = h_ref.shape[1]

    def copy(slot, j, tok):
        dst = buf_ref.at[slot, pl.ds(pl.multiple_of(j * ROW_SLAB, 8), nc), :]
        return pltpu.make_async_copy(h_ref.at[tok], dst, sem.at[slot])

    def start_all(idx_ref, slot):
        def body(j2, c):
            for u in range(2):
                j = j2 * 2 + u
                copy(slot, j, idx_ref[0, 0, j]).start(priority=u)
            return c
        lax.fori_loop(0, rb // 2, body, 0, unroll=4)

    @pl.when(i == 0)
    def _():
        start_all(cur_ref, 0)

    @pl.when(jnp.logical_and(i + 1 < n, i + 1 < nblk_ref[0]))
    def _():
        start_all(nxt_ref, (i + 1) % 2)

    @pl.when(i < nblk_ref[0])
    def _():
        slot = i % 2

        def drain(j, c):
            copy(slot, j, 0).wait()
            return c
        lax.fori_loop(0, rb, drain, 0, unroll=8)
        for c in range(nc):
            o_ref[:, c * LANES:(c + 1) * LANES] = (
                buf_ref[slot, pl.ds(c, rb, stride=ROW_SLAB), :].astype(o_ref.dtype))

    @pl.when(i >= nblk_ref[0])
    def _():
        o_ref[...] = jnp.zeros(o_ref.shape, o_ref.dtype)


def _gather_rows(h3, row_tok, n_used_blocks):
    n_rows = row_tok.shape[0]
    _, nc, _ = h3.shape
    rb = MOE_BLOCK
    nb = n_rows // rb
    tok_blocks = row_tok.reshape(nb, 1, rb)
    return pl.pallas_call(
        _gather_kernel,
        grid_spec=pltpu.PrefetchScalarGridSpec(
            num_scalar_prefetch=1,
            grid=(nb,),
            in_specs=[
                pl.BlockSpec((1, 1, rb), lambda i, nu: (i, 0, 0), memory_space=pltpu.SMEM),
                pl.BlockSpec((1, 1, rb), lambda i, nu: (jnp.minimum(i + 1, nb - 1), 0, 0),
                             memory_space=pltpu.SMEM),
                pl.BlockSpec(memory_space=pl.ANY),
            ],
            out_specs=pl.BlockSpec((rb, nc * LANES), lambda i, nu: (i, 0)),
            scratch_shapes=[pltpu.VMEM((2, rb * ROW_SLAB, LANES), h3.dtype),
                            pltpu.SemaphoreType.DMA((2,))],
        ),
        out_shape=jax.ShapeDtypeStruct((n_rows, nc * LANES), BF16),
        compiler_params=_params(("arbitrary",)),
        name="moe_gather",
    )(n_used_blocks, tok_blocks, tok_blocks, h3)


def _expert_ring(bstart_ref, bcount_ref, total_ref, in_copy, out_copy, compute, zero_slot, n_blocks, n_pass):
    n = pl.program_id(0)
    e = pl.program_id(1)
    b0 = bstart_ref[e]
    nb = bcount_ref[e]
    total = total_ref[0]
    last_q = pl.num_programs(0) * total - 1

    @pl.when(jnp.logical_and(nb > 0, jnp.logical_and(n == 0, b0 == 0)))
    def _():
        in_copy(0, 0).start()

    def body(b, carry):
        gb = b0 + b
        q = n * total + gb
        slot = q % 2

        @pl.when(q < last_q)
        def _():
            in_copy(1 - slot, jnp.where(gb + 1 < total, gb + 1, 0)).start()

        in_copy(slot, 0).wait()

        @pl.when(q >= 2)
        def _():
            out_copy(slot, 0, 0).wait()

        compute(slot)
        out_copy(slot, gb, n).start()

        @pl.when(q == last_q)
        def _():
            @pl.when(q >= 1)
            def _():
                out_copy(1 - slot, 0, 0).wait()
            out_copy(slot, 0, 0).wait()

        return carry

    lax.fori_loop(0, nb, body, 0)

    @pl.when(jnp.logical_and(n == pl.num_programs(0) - 1, e == pl.num_programs(1) - 1))
    def _():
        zero_slot(0)

        def fill(gb, carry):
            for n_out in range(n_pass):
                cp = out_copy(0, gb, n_out)
                cp.start()
                cp.wait()
            return carry

        lax.fori_loop(total, n_blocks, fill, 0)


def _moe_up_kernel(bstart_ref, bcount_ref, total_ref, xs_ref, wg_ref, wu_ref, o_ref,
                   xbuf_ref, obuf_ref, wgb_ref, wub_ref, sem_in, sem_out):
    blk = xbuf_ref.shape[1]
    tn = obuf_ref.shape[2]

    @pl.when(bcount_ref[pl.program_id(1)] > 0)
    def _():
        wgb_ref[...] = wg_ref[...].astype(BF16)
        wub_ref[...] = wu_ref[...].astype(BF16)

    def in_copy(slot, gb):
        rows = pl.ds(pl.multiple_of(gb * blk, blk), blk)
        return pltpu.make_async_copy(xs_ref.at[rows], xbuf_ref.at[slot], sem_in.at[slot])

    def out_copy(slot, gb, n):
        rows = pl.ds(pl.multiple_of(gb * blk, blk), blk)
        cols = pl.ds(pl.multiple_of(n * tn, tn), tn)
        return pltpu.make_async_copy(obuf_ref.at[slot], o_ref.at[rows, cols], sem_out.at[slot])

    def compute(slot):
        x = xbuf_ref[slot]
        g = jnp.dot(x, wgb_ref[...], preferred_element_type=F32)
        u = jnp.dot(x, wub_ref[...], preferred_element_type=F32)
        obuf_ref[slot] = (g * jax.nn.sigmoid(g) * u).astype(obuf_ref.dtype)

    def zero_slot(slot):
        obuf_ref[slot] = jnp.zeros(obuf_ref.shape[1:], obuf_ref.dtype)

    _expert_ring(bstart_ref, bcount_ref, total_ref, in_copy, out_copy, compute, zero_slot,
                 o_ref.shape[0] // blk, o_ref.shape[1] // tn)


def _moe_down_kernel(bstart_ref, bcount_ref, total_ref, h_ref, wd_ref, o_ref,
                     hbuf_ref, obuf_ref, wdb_ref, slab_ref, sem_in, sem_out):
    blk = hbuf_ref.shape[1]
    nc = obuf_ref.shape[2]

    @pl.when(bcount_ref[pl.program_id(1)] > 0)
    def _():
        wdb_ref[...] = wd_ref[...].astype(BF16)

    def in_copy(slot, gb):
        rows = pl.ds(pl.multiple_of(gb * blk, blk), blk)
        return pltpu.make_async_copy(h_ref.at[rows], hbuf_ref.at[slot], sem_in.at[slot])

    def out_copy(slot, gb, n):
        rows = pl.ds(pl.multiple_of(gb * blk, blk), blk)
        slabs = pl.ds(pl.multiple_of(n * nc, nc), nc)
        return pltpu.make_async_copy(obuf_ref.at[slot], o_ref.at[rows, slabs], sem_out.at[slot])

    stride = slab_ref.shape[0] // blk

    @pl.when(jnp.logical_and(pl.program_id(0) == 0, pl.program_id(1) == 0))
    def _():
        slab_ref[...] = jnp.zeros(slab_ref.shape, slab_ref.dtype)

    def compute(slot):
        acc = jnp.dot(hbuf_ref[slot], wdb_ref[...], preferred_element_type=F32)
        for c in range(nc):
            slab_ref[pl.ds(c, blk, stride=stride), :] = acc[:, c * LANES:(c + 1) * LANES]
        obuf_ref[slot] = slab_ref[...].reshape(blk, stride, LANES)[:, :nc, :]

    def zero_slot(slot):
        obuf_ref[slot] = jnp.zeros(obuf_ref.shape[1:], obuf_ref.dtype)

    _expert_ring(bstart_ref, bcount_ref, total_ref, in_copy, out_copy, compute, zero_slot,
                 o_ref.shape[0] // blk, o_ref.shape[1] // nc)


def _moe_experts(xs, bstart, bcount, total, w_gate, w_up, w_down, layer):
    n_rows, d = xs.shape
    n_exp, de = w_gate.shape[1], w_gate.shape[-1]
    blk = MOE_BLOCK
    tn1, tn2 = min(512, de), min(2048, d)
    any_spec = pl.BlockSpec(memory_space=pl.ANY)
    wspec = lambda k, tn: pl.BlockSpec((None, None, k, tn), lambda n, e, bs, bc, tot: (layer, e, 0, n))
    dma2 = pltpu.SemaphoreType.DMA((2,))
    hmid = pl.pallas_call(
        _moe_up_kernel,
        grid_spec=pltpu.PrefetchScalarGridSpec(
            num_scalar_prefetch=3,
            grid=(de // tn1, n_exp),
            in_specs=[any_spec, wspec(d, tn1), wspec(d, tn1)],
            out_specs=any_spec,
            scratch_shapes=[pltpu.VMEM((2, blk, d), BF16), pltpu.VMEM((2, blk, tn1), BF16),
                            pltpu.VMEM((d, tn1), BF16), pltpu.VMEM((d, tn1), BF16), dma2, dma2],
        ),
        out_shape=jax.ShapeDtypeStruct((n_rows, de), BF16),
        compiler_params=_params(("arbitrary", "arbitrary")),
        name="moe_up",
    )(bstart, bcount, total, xs, w_gate, w_up)
    nc2 = tn2 // LANES
    return pl.pallas_call(
        _moe_down_kernel,
        grid_spec=pltpu.PrefetchScalarGridSpec(
            num_scalar_prefetch=3,
            grid=(d // tn2, n_exp),
            in_specs=[any_spec, wspec(de, tn2)],
            out_specs=any_spec,
            scratch_shapes=[pltpu.VMEM((2, blk, de), BF16), pltpu.VMEM((2, blk, nc2, LANES), F32),
                            pltpu.VMEM((de, tn2), BF16), pltpu.VMEM((blk * (nc2 + 8), LANES), F32),
                            dma2, dma2],
        ),
        out_shape=jax.ShapeDtypeStruct((n_rows, d // LANES, LANES), F32),
        compiler_params=_params(("arbitrary", "arbitrary")),
        name="moe_down",
    )(bstart, bcount, total, hmid, w_down)


def _combine_kernel(cur_ref, nxt_ref, y_ref, x_ref, gate_ref, gt_ref, gf_ref, *rest, final_norm, split):
    out_refs, (buf_ref, sem) = rest[:-2], rest[-2:]
    i = pl.program_id(0)
    n = pl.num_programs(0)
    tm = x_ref.shape[0]
    rows = TOP_K * tm
    nc = y_ref.shape[1]

    def copy(slot, j, r):
        dst = buf_ref.at[slot, pl.ds(pl.multiple_of(j * ROW_SLAB, 8), nc), :]
        return pltpu.make_async_copy(y_ref.at[r], dst, sem.at[slot])

    def start_all(idx_ref, slot):
        def body(j2, c):
            for u in range(2):
                j = j2 * 2 + u
                copy(slot, j, idx_ref[0, 0, j]).start(priority=u)
            return c
        lax.fori_loop(0, rows // 2, body, 0, unroll=4)

    @pl.when(i == 0)
    def _():
        start_all(cur_ref, 0)

    @pl.when(i + 1 < n)
    def _():
        start_all(nxt_ref, (i + 1) % 2)

    slot = i % 2

    def drain(j, c):
        copy(slot, j, 0).wait()
        return c
    lax.fori_loop(0, rows, drain, 0, unroll=8)

    gate = gate_ref[...]
    lane = lax.broadcasted_iota(jnp.int32, gate.shape, 1)
    w1 = jnp.sum(jnp.where(lane == 0, gate, 0.0), axis=-1, keepdims=True)
    w2 = jnp.sum(jnp.where(lane == 1, gate, 0.0), axis=-1, keepdims=True)
    moe = jnp.concatenate(
        [w1 * buf_ref[slot, pl.ds(c, tm, stride=ROW_SLAB), :]
         + w2 * buf_ref[slot, pl.ds(tm * ROW_SLAB + c, tm, stride=ROW_SLAB), :]
         for c in range(nc)], axis=-1)
    out = x_ref[...] + gt_ref[...] * moe
    if final_norm:
        out = _rms(out, gf_ref[...])
    if split is None:
        out_refs[0][...] = out
    else:
        @pl.when(i < split)
        def _():
            out_refs[0][...] = out

        @pl.when(i >= split)
        def _():
            out_refs[1][...] = out


def _combine(y, dest_blocks, x, gates, gt, g_final, lay, final_norm, split_groups=False):
    t, d = x.shape
    tm = 128
    nb = t // tm
    split = lay.tp // tm if split_groups else None
    kern = functools.partial(_combine_kernel, final_norm=final_norm, split=split)
    if split_groups:
        out_specs = [pl.BlockSpec((tm, d), lambda i: (jnp.minimum(i, split - 1), 0)),
                     pl.BlockSpec((tm, d), lambda i: (jnp.maximum(i - split, 0), 0))]
        out_shape = [jax.ShapeDtypeStruct((lay.tp, d), F32), jax.ShapeDtypeStruct((t - lay.tp, d), F32)]
    else:
        out_specs = pl.BlockSpec((tm, d), lambda i: (i, 0))
        out_shape = jax.ShapeDtypeStruct((t, d), F32)
    return pl.pallas_call(
        kern,
        grid=(nb,),
        in_specs=[
            pl.BlockSpec((1, 1, 2 * tm), lambda i: (i, 0, 0), memory_space=pltpu.SMEM),
            pl.BlockSpec((1, 1, 2 * tm), lambda i: (jnp.minimum(i + 1, nb - 1), 0, 0),
                         memory_space=pltpu.SMEM),
            pl.BlockSpec(memory_space=pl.ANY),
            pl.BlockSpec((tm, d), lambda i: (i, 0)),
            pl.BlockSpec((tm, LANES), lambda i: (i, 0)),
            pl.BlockSpec((None, 1, d), lambda i: (lay.seq_of(i * tm), 0, 0)),
            pl.BlockSpec((1, d), lambda i: (0, 0)),
        ],
        out_specs=out_specs,
        out_shape=out_shape,
        scratch_shapes=[pltpu.VMEM((2, TOP_K * tm * ROW_SLAB, LANES), y.dtype),
                        pltpu.SemaphoreType.DMA((2,))],
        compiler_params=_params(("arbitrary",)),
        name="moe_combine",
    )(dest_blocks, dest_blocks, y, x, gates, gt, g_final.reshape(1, d))


def _t5_bucket(rel):
    half = NUM_BUCKETS // 2
    max_exact = half // 2
    n = jnp.abs(rel)
    large = max_exact + (jnp.log(jnp.maximum(n, max_exact).astype(F32) / max_exact)
                         / math.log(MAX_DISTANCE / max_exact) * (half - max_exact)).astype(jnp.int32)
    large = jnp.minimum(large, half - 1)
    return jnp.where(rel > 0, half, 0) + jnp.where(n < max_exact, n, large)


def _rope_tables(s):
    rows = s // GRID_W
    row = jnp.repeat(jnp.arange(rows, dtype=F32), GRID_W)
    col = jnp.tile(jnp.arange(GRID_W, dtype=F32), rows)
    inv_freq = jnp.exp(-math.log(ROPE_THETA) * jnp.arange(0, AXIS_DIM, 2, dtype=F32) / AXIS_DIM)
    ang_r = row[:, None] * inv_freq[None, :]
    ang_c = col[:, None] * inv_freq[None, :]
    cos = jnp.concatenate([jnp.cos(ang_r)] * 2 + [jnp.cos(ang_c)] * 2, axis=-1)
    sin = jnp.concatenate([-jnp.sin(ang_r), jnp.sin(ang_r), -jnp.sin(ang_c), jnp.sin(ang_c)], axis=-1)
    return cos, sin


def _toeplitz_rel(rows, cols):
    j = jnp.arange(rows + cols, dtype=jnp.int32)
    return jnp.where(j < cols, j, j - (rows + cols))


def _toeplitz(w, rows, cols):
    period = rows + cols
    flat = jnp.tile(w, (1,) * (w.ndim - 1) + (rows,))[..., :rows * (period - 1)]
    return flat.reshape(w.shape[:-1] + (rows, period - 1))[..., :cols]


def _dilated_tables(bias_tab, kv, dil):
    offs = (jnp.arange(2 * B_HALF + 1, dtype=jnp.int32) - B_HALF) * dil
    bias = bias_tab[_t5_bucket(offs)].astype(F32)
    tq = 128
    shift = jnp.array([-B_HALF, 0, -2 * B_HALF], jnp.int32)[:, None]
    rel = _toeplitz_rel(tq, 2 * tq)[None, :] + shift
    vals = jnp.where((jnp.abs(rel) <= B_HALF)[..., None],
                     bias[jnp.clip(rel + B_HALF, 0, 2 * B_HALF)], NEG_INF)
    tab = _toeplitz(jnp.moveaxis(vals, -1, 0), tq, 2 * tq)
    return jnp.swapaxes(tab.reshape(kv, 3, 3, tq, 2 * tq), 1, 2)


def _diff_tables(bias_tab, t):
    rel = _toeplitz_rel(t, t)[None, :] + jnp.array([-t, 0, t], jnp.int32)[:, None]
    near = _toeplitz(jnp.moveaxis(bias_tab[_t5_bucket(rel)].astype(F32), -1, 0), t, t)
    far = bias_tab[_t5_bucket(jnp.array([-2 * t, 2 * t], jnp.int32))].astype(F32).T
    tile = lambda c: jnp.broadcast_to(c[:, None, None, None], (c.shape[0], 1, t, t))
    return jnp.concatenate([tile(far[:, 0]), near, tile(far[:, 1])], axis=1) * LOG2E


def _routing(idx, blk):
    t = idx.shape[0]
    n_assign = t * TOP_K
    flat_e = idx.reshape(-1)
    experts = jnp.arange(N_EXPERTS, dtype=jnp.int32)[None, None, :]
    onehot = jnp.sum((idx[:, :, None] == experts).astype(jnp.int32), axis=1)
    csum = jnp.cumsum(onehot, axis=0)
    rank = (jnp.take_along_axis(csum, idx, axis=1) - 1).reshape(-1)
    counts = csum[-1]
    padded = (counts + blk - 1) // blk * blk
    pad_end = jnp.cumsum(padded)
    pad_start = pad_end - padded
    dest = (pad_start[flat_e] + rank).astype(jnp.int32)
    n_rows = n_assign + N_EXPERTS * blk
    flat_tok = jnp.arange(n_assign, dtype=jnp.int32) // TOP_K
    row_tok = jnp.zeros((n_rows,), jnp.int32).at[dest].set(flat_tok)
    n_used = (pad_end[-1] // blk).astype(jnp.int32).reshape(1)
    bstart = (pad_start // blk).astype(jnp.int32)
    bcount = (padded // blk).astype(jnp.int32)
    return dest.reshape(t, TOP_K), row_tok, bstart, bcount, n_used


def _dims(d):
    slots = d // HEAD_DIM
    a_heads = 3 * slots // 8
    a_kv = a_heads // 3
    b_heads = 3 * slots // 8
    b_kv = b_heads // 3
    c_heads = slots // 8
    c_w = c_heads * 2 * HEAD_DIM
    offs, acc = [], 0
    for n in (b_heads, b_kv, b_kv, a_heads, a_kv, a_kv):
        offs.append(acc)
        acc += n * HEAD_DIM
    for n in (c_w, c_w, c_w):
        offs.append(acc)
        acc += n
    names = ("qb_off", "kb_off", "vb_off", "qa_off", "ka_off", "va_off", "qc_off", "kc_off", "vc_off")
    out = dict(zip(names, offs))
    out.update(a_heads=a_heads, a_kv=a_kv, a_grp=3, b_heads=b_heads, b_kv=b_kv, c_heads=c_heads,
               a_w=a_heads * HEAD_DIM, b_w=b_heads * HEAD_DIM, c_w=c_w, in_w=acc)
    return out


def kernel(x_prompt, x_sample, c_prompt, c_sample, rel_bias, w_ada, b_ada, g_norm1, w_in, g_qk, lam_c,
           g_out, w_out, g_norm2, w_group, b_group, w_router, b_router, w_gate, w_up, w_down, g_final):
    bp, sp, d = x_prompt.shape
    bs, ss, _ = x_sample.shape
    depth = w_in.shape[0]
    lay = _Layout(bp, sp, bs, ss)
    dims = _dims(d)
    t = lay.t
    x = jnp.concatenate([x_prompt.reshape(bp * sp, d), x_sample.reshape(bs * ss, d)], axis=0)

    c_all = jnp.concatenate([c_prompt, c_sample], axis=0)
    pad = (-c_all.shape[0]) % 8
    c_pad = jnp.pad(c_all, ((0, pad), (0, 0)))
    mod = _ada_mod(c_pad, w_ada, b_ada)[:, :lay.nseq].reshape(depth, lay.nseq, 6, 1, d)

    rope_cos, rope_sin = _rope_tables(max(sp, ss))
    qk_w = dims["va_off"] - dims["qa_off"]
    b_tabs = [_dilated_tables(rel_bias[:, :dims["b_heads"]], dims["b_kv"], dil) for _, dil in B_BRANCHES]
    c_t = min(512, sp, ss)
    c_tab = _diff_tables(rel_bias[:, dims["b_heads"]:], c_t)

    for l in range(depth):
        lambda_init = 0.8 - 0.6 * math.exp(-0.3 * l)
        sh1, sc1, gt1, sh2, sc2, gt2 = (mod[l, :, i] for i in range(6))

        a_end = dims["a_w"] + 2 * dims["a_kv"] * HEAD_DIM
        b_end = a_end + dims["qa_off"]
        w_l = jnp.concatenate([w_in[l, :, a_end:b_end], w_in[l, :, :a_end], w_in[l, :, b_end:]],
                              axis=1).astype(BF16)
        p, *p_res = _inproj(_normmod(x, g_norm1[l], sc1, sh1, lay), w_l, lay, dims)
        p_by_dil = [p.reshape(1, t, p.shape[1])] + p_res
        g_row = jnp.concatenate([jnp.tile(g_qk[l, 0], dims["a_heads"]),
                                 jnp.tile(g_qk[l, 1], dims["a_kv"])]).reshape(1, qk_w)
        qk = _aprep(p, g_row, rope_cos, rope_sin, lay, dims["qa_off"], qk_w)

        lam = lam_c[l].astype(F32)
        lam_val = (jnp.exp(jnp.sum(lam[0] * lam[1])) - jnp.exp(jnp.sum(lam[2] * lam[3])) + lambda_init)
        lam_val = lam_val.reshape(1).astype(F32)
        g_c = g_out[l, dims["a_w"] + dims["b_w"]:].reshape(dims["c_heads"], 1, 2 * HEAD_DIM)

        if l == 0:
            oa = jnp.zeros((t, dims["a_w"]), F32)
            oc = jnp.zeros((t, dims["c_w"]), BF16)
            obs = [jnp.zeros((dil, t // dil, dims["b_w"]), F32) for _, dil in B_BRANCHES]
            lses = [jnp.zeros((dil, t // dil, dims["b_kv"] * LANES), F32) for _, dil in B_BRANCHES]
        for group in lay.groups:
            oa = _flash_a(qk, p, oa, group, dims)
            oc = _flash_c(p, lam_val, c_tab, g_c, oc, group, dims, 1.0 - lambda_init)
            for n, (_, dil) in enumerate(B_BRANCHES):
                obs[n], lses[n] = _dilated_branch(p_by_dil[n], b_tabs[n], obs[n], lses[n], group, dims, dil)
        mixed = _mix(oa, obs, lses, oc, g_out[l], dims)
        x = _outproj(mixed, w_out[l].astype(BF16), x, gt1, lay)

        w_r = jnp.concatenate([w_group[l], w_router[l],
                               jnp.zeros((d, LANES - N_GROUPS - N_EXPERTS), F32)], axis=1)
        w_hi = w_r.astype(BF16)
        w_lo = (w_r - w_hi.astype(F32)).astype(BF16)
        b_row = jnp.concatenate([b_group[l], b_router[l],
                                 jnp.zeros((LANES - N_GROUPS - N_EXPERTS,), F32)]).reshape(1, LANES)
        h2, gates, idx = _router(x, g_norm2[l], sc2, sh2, w_hi, w_lo, b_row.astype(F32), lay)
        dest, row_tok, bstart, bcount, n_used = _routing(idx[:, :TOP_K], MOE_BLOCK)
        xs = _gather_rows(h2.reshape(t, d // LANES, LANES), row_tok, n_used)
        y = _moe_experts(xs, bstart, bcount, n_used, w_gate, w_up, w_down, l)
        tmc = 128
        dest_blocks = dest.reshape(t // tmc, tmc, TOP_K).transpose(0, 2, 1).reshape(t // tmc, 1, TOP_K * tmc)
        last = l == depth - 1
        x = _combine(y, dest_blocks, x, gates, gt2, g_final, lay, final_norm=last, split_groups=last)

    return (x[0].reshape(bp, sp, d), x[1].reshape(bs, ss, d))
```
